```python
import functools
import jax, jax.numpy as jnp
from jax import lax
import numpy as np

D_MODEL = 2048
BATCH = 4
SEQ = 8192
DEPTH = 2
DEC_BATCH = 8
DEC_SEQ = 16
PAST_LEN = 4096

CHUNK = 64
N_HEADS = 12
N_KV_HEADS = 4
HEAD_DIM = 128
GROUP = N_HEADS // N_KV_HEADS
N_MEM = 256
N_MEM_HEADS = 4
IDX_HEADS = 8
IDX_DIM = 64
TOPK_MAX = 256
BAND_CHUNKS = 8
WIN_B = BAND_CHUNKS * CHUNK
REL_CLIP = 128
D_FF = 5632
ROPE_THETA = 10000.0
LN_EPS = 1e-5
ALPHA = (2 * DEPTH) ** 0.25
BETA = (8 * DEPTH) ** -0.25
N_A_LAYERS = (DEPTH + 1) // 2
N_B_LAYERS = DEPTH // 2
Q_W = N_HEADS * HEAD_DIM
KV_W = N_KV_HEADS * HEAD_DIM
MQ_W = N_MEM_HEADS * HEAD_DIM
IDXQ_W = IDX_HEADS * IDX_DIM
PROJ_A = Q_W + 2 * KV_W + MQ_W + IDXQ_W + IDX_DIM + IDX_HEADS
PROJ_B = Q_W + 2 * KV_W + MQ_W
O_W = Q_W + MQ_W
SPLIT_A = tuple(int(c) for c in np.cumsum([Q_W, KV_W, KV_W, MQ_W, IDXQ_W, IDX_DIM]))
SPLIT_B = tuple(int(c) for c in np.cumsum([Q_W, KV_W, KV_W]))
ATTN_SCALE = HEAD_DIM ** -0.5
IDX_W_SCALE = (IDX_HEADS * IDX_DIM) ** -0.5
NEG_INF = -1e30

kernel_name = 'hybrid_streaming_dsa_chunkband'


def layer_norm(x, g, b):
    xf = x.astype(jnp.float32)
    mu = xf.mean(-1, keepdims=True)
    var = jnp.square(xf - mu).mean(-1, keepdims=True)
    return ((xf - mu) * lax.rsqrt(var + LN_EPS)).astype(x.dtype) * g + b


def rope(x, pos):
    half = x.shape[-1] // 2
    inv_freq = ROPE_THETA ** (-jnp.arange(half, dtype=jnp.float32) / half)
    ang = pos.astype(jnp.float32)[:, None] * inv_freq[None, :]
    cos, sin = jnp.cos(ang)[:, None, :], jnp.sin(ang)[:, None, :]
    xf = x.astype(jnp.float32)
    x1, x2 = xf[..., :half], xf[..., half:]
    return jnp.concatenate([x1 * cos - x2 * sin, x2 * cos + x1 * sin], -1).astype(x.dtype)


def swiglu(x, w_in, w_out):
    g, u = jnp.split(x @ w_in, 2, axis=-1)
    return (jax.nn.silu(g) * u) @ w_out


def mem_attend(qm, mk, mv):
    s = jnp.einsum('bthd,bmhd->bhtm', qm, mk).astype(jnp.float32) * ATTN_SCALE
    p = jax.nn.softmax(s, axis=-1).astype(mv.dtype)
    return jnp.einsum('bhtm,bmhd->bthd', p, mv)


def dsa_attend(q, qi, wi, k, v, ki, q_pos, k_pos, topk):
    B, T = q.shape[:2]
    qb = min(128, T)
    nb = T // qb
    k_chunk = k_pos // CHUNK
    gather = jax.vmap(lambda a, idx: a[idx])
    kif = ki.astype(jnp.float32)

    def split(a):
        return a.reshape((B, nb, qb) + a.shape[2:]).swapaxes(0, 1)

    def block(args):
        qbk, qib, wib, qpos = args
        q_chunk = qpos // CHUNK
        adm = k_chunk[None, :] <= q_chunk[:, None]
        rel = jax.nn.relu(jnp.einsum('bqhd,bsd->bqhs', qib.astype(jnp.float32), kif))
        idx_score = jnp.einsum('bqh,bqhs->bqs', wib.astype(jnp.float32) * IDX_W_SCALE, rel)
        idx_score = jnp.where(adm[None], idx_score, NEG_INF)
        _, sel = lax.top_k(idx_score, topk)
        valid = jnp.take(k_chunk, sel) <= q_chunk[None, :, None]
        ks = gather(k, sel)
        vs = gather(v, sel)
        qg = qbk.reshape(B, qb, N_KV_HEADS, GROUP, HEAD_DIM)
        s = jnp.einsum('bqgrd,bqkgd->bqgrk', qg, ks).astype(jnp.float32) * ATTN_SCALE
        s = jnp.where(valid[:, :, None, None, :], s, NEG_INF)
        p = jax.nn.softmax(s, axis=-1).astype(vs.dtype)
        o = jnp.einsum('bqgrk,bqkgd->bqgrd', p, vs)
        return o.reshape(B, qb, N_HEADS, HEAD_DIM)

    out = lax.map(block, (split(q), split(qi), split(wi), q_pos.reshape(nb, qb)))
    return out.swapaxes(0, 1).reshape(B, T, N_HEADS, HEAD_DIM)


def band_attend(q, k_full, v_full, rel_bias, q_pos, k_pos):
    B, T = q.shape[:2]
    P = k_full.shape[1] - T
    qc = min(CHUNK, T)
    nc = T // qc
    band = P + qc

    def block(c):
        qs = lax.dynamic_slice_in_dim(q, c * qc, qc, axis=1)
        ks = lax.dynamic_slice_in_dim(k_full, c * qc, band, axis=1)
        vs = lax.dynamic_slice_in_dim(v_full, c * qc, band, axis=1)
        qp = lax.dynamic_slice_in_dim(q_pos, c * qc, qc)
        kp = lax.dynamic_slice_in_dim(k_pos, c * qc, band)
        rel = jnp.clip(qp[:, None] - kp[None, :], -REL_CLIP, REL_CLIP) + REL_CLIP
        bias = rel_bias[:, rel].reshape(N_KV_HEADS, GROUP, qc, band)
        qg = qs.reshape(B, qc, N_KV_HEADS, GROUP, HEAD_DIM)
        s = jnp.einsum('bqgrd,bkgd->bgrqk', qg, ks).astype(jnp.float32) * ATTN_SCALE
        s = s + bias[None].astype(jnp.float32)
        s = jnp.where((kp >= 0)[None, None, None, None, :], s, NEG_INF)
        p = jax.nn.softmax(s, axis=-1).astype(vs.dtype)
        o = jnp.einsum('bgrqk,bkgd->bqgrd', p, vs)
        return o.reshape(B, qc, N_HEADS, HEAD_DIM)

    out = lax.map(block, jnp.arange(nc))
    return out.swapaxes(0, 1).reshape(B, T, N_HEADS, HEAD_DIM)


def mixer_a(x, p0, past, mk, mv, w_in, w_o):
    B, T, _ = x.shape
    pos = p0 + jnp.arange(T)
    q, k, v, qm, qi, ki, wi = jnp.split(x @ w_in, SPLIT_A, axis=-1)
    q = rope(q.reshape(B, T, N_HEADS, HEAD_DIM), pos)
    k = rope(k.reshape(B, T, N_KV_HEADS, HEAD_DIM), pos)
    v = v.reshape(B, T, N_KV_HEADS, HEAD_DIM)
    qi = rope(qi.reshape(B, T, IDX_HEADS, IDX_DIM), pos)
    ki = rope(ki.reshape(B, T, 1, IDX_DIM), pos)[:, :, 0]
    if past is None:
        k_all, v_all, ki_all = k, v, ki
    else:
        k_all = jnp.concatenate([past[0], k], axis=1)
        v_all = jnp.concatenate([past[1], v], axis=1)
        ki_all = jnp.concatenate([past[2], ki], axis=1)
    L = k_all.shape[1]
    topk = min(TOPK_MAX, L // 4)
    o = dsa_attend(q, qi, wi, k_all, v_all, ki_all, pos, jnp.arange(L), topk)
    om = mem_attend(qm.reshape(B, T, N_MEM_HEADS, HEAD_DIM), mk, mv)
    y = jnp.concatenate([o.reshape(B, T, Q_W), om.reshape(B, T, MQ_W)], axis=-1) @ w_o
    return y, (k, v, ki)


def mixer_b(x, p0, past, mk, mv, w_in, w_o, rel_bias):
    B, T, _ = x.shape
    pos = p0 + jnp.arange(T)
    q, k, v, qm = jnp.split(x @ w_in, SPLIT_B, axis=-1)
    q = q.reshape(B, T, N_HEADS, HEAD_DIM)
    k = k.reshape(B, T, N_KV_HEADS, HEAD_DIM)
    v = v.reshape(B, T, N_KV_HEADS, HEAD_DIM)
    if past is None:
        pk = jnp.zeros((B, WIN_B, N_KV_HEADS, HEAD_DIM), x.dtype)
        pv = pk
    else:
        pk, pv = past
    P = pk.shape[1]
    k_all = jnp.concatenate([pk, k], axis=1)
    v_all = jnp.concatenate([pv, v], axis=1)
    k_pos = jnp.arange(P + T) + (p0 - P)
    o = band_attend(q, k_all, v_all, rel_bias, pos, k_pos)
    om = mem_attend(qm.reshape(B, T, N_MEM_HEADS, HEAD_DIM), mk, mv)
    y = jnp.concatenate([o.reshape(B, T, Q_W), om.reshape(B, T, MQ_W)], axis=-1) @ w_o
    if past is None:
        n_keep = min(WIN_B, T)
        new_state = (k[:, T - n_keep:], v[:, T - n_keep:])
    else:
        new_state = (k_all[:, T:], v_all[:, T:])
    return y, new_state


def macaron_layer(x, mixer, w1_in, w1_out, w2_in, w2_out, g, b):
    x = layer_norm(ALPHA * x + 0.5 * swiglu(x, w1_in, w1_out), g[0], b[0])
    y, new_state = mixer(x)
    x = layer_norm(ALPHA * x + y, g[1], b[1])
    x = layer_norm(ALPHA * x + 0.5 * swiglu(x, w2_in, w2_out), g[2], b[2])
    return x, new_state


def setup_inputs(seed: int = 0) -> dict:
    key = jax.random.key(seed)
    ks = jax.random.split(key, 22)
    b_buf = min(WIN_B, PAST_LEN)

    def nrm(k, shape, scale):
        return jax.random.normal(k, shape, jnp.float32) * scale

    return {
        'x_prompt': nrm(ks[0], (BATCH, SEQ, D_MODEL), 1.0),
        'x_sample': nrm(ks[1], (DEC_BATCH, DEC_SEQ, D_MODEL), 1.0),
        'mem_prompt': nrm(ks[2], (BATCH, N_MEM, D_MODEL), 1.0),
        'cache_a_k': nrm(ks[3], (N_A_LAYERS, DEC_BATCH, PAST_LEN, N_KV_HEADS, HEAD_DIM), 1.0),
        'cache_a_v': nrm(ks[4], (N_A_LAYERS, DEC_BATCH, PAST_LEN, N_KV_HEADS, HEAD_DIM), 1.0),
        'cache_a_kidx': nrm(ks[5], (N_A_LAYERS, DEC_BATCH, PAST_LEN, IDX_DIM), 1.0),
        'cache_b_k': nrm(ks[6], (N_B_LAYERS, DEC_BATCH, b_buf, N_KV_HEADS, HEAD_DIM), 1.0),
        'cache_b_v': nrm(ks[7], (N_B_LAYERS, DEC_BATCH, b_buf, N_KV_HEADS, HEAD_DIM), 1.0),
        'cache_mem_k': nrm(ks[8], (DEPTH, DEC_BATCH, N_MEM, N_MEM_HEADS, HEAD_DIM), 1.0),
        'cache_mem_v': nrm(ks[9], (DEPTH, DEC_BATCH, N_MEM, N_MEM_HEADS, HEAD_DIM), 1.0),
        'w_in_a': nrm(ks[10], (N_A_LAYERS, D_MODEL, PROJ_A), D_MODEL ** -0.5),
        'w_in_b': nrm(ks[11], (N_B_LAYERS, D_MODEL, PROJ_B), D_MODEL ** -0.5),
        'w_o': nrm(ks[12], (DEPTH, O_W, D_MODEL), BETA * O_W ** -0.5),
        'rel_bias': nrm(ks[13], (N_B_LAYERS, N_HEADS, 2 * REL_CLIP + 1), 0.5),
        'w_mem_kv': nrm(ks[14], (DEPTH, D_MODEL, 2 * MQ_W), D_MODEL ** -0.5),
        'w_ffn1_in': nrm(ks[15], (DEPTH, D_MODEL, 2 * D_FF), D_MODEL ** -0.5),
        'w_ffn1_out': nrm(ks[16], (DEPTH, D_FF, D_MODEL), BETA * D_FF ** -0.5),
        'w_ffn2_in': nrm(ks[17], (DEPTH, D_MODEL, 2 * D_FF), D_MODEL ** -0.5),
        'w_ffn2_out': nrm(ks[18], (DEPTH, D_FF, D_MODEL), BETA * D_FF ** -0.5),
        'ln_g': 1.0 + nrm(ks[19], (DEPTH, 3, D_MODEL), 0.02),
        'ln_b': nrm(ks[20], (DEPTH, 3, D_MODEL), 0.02),
    }


def reference(x_prompt, x_sample, mem_prompt, cache_a_k, cache_a_v, cache_a_kidx,
              cache_b_k, cache_b_v, cache_mem_k, cache_mem_v,
              w_in_a, w_in_b, w_o, rel_bias, w_mem_kv,
              w_ffn1_in, w_ffn1_out, w_ffn2_in, w_ffn2_out, ln_g, ln_b):
    past_len = cache_a_k.shape[2]
    bp = mem_prompt.shape[0]
    mem_kv = jnp.einsum('bmd,ldc->lbmc', mem_prompt, w_mem_kv)
    mk_p, mv_p = jnp.split(mem_kv, 2, axis=-1)
    mk_p = mk_p.reshape(DEPTH, bp, N_MEM, N_MEM_HEADS, HEAD_DIM)
    mv_p = mv_p.reshape(DEPTH, bp, N_MEM, N_MEM_HEADS, HEAD_DIM)
    xp, xs = x_prompt, x_sample
    st_a_p, st_b_p, st_a_s, st_b_s = [], [], [], []
    for i in range(DEPTH):
        j = i // 2
        ffn = (w_ffn1_in[i], w_ffn1_out[i], w_ffn2_in[i], w_ffn2_out[i], ln_g[i], ln_b[i])
        if i % 2 == 0:
            mix_p = functools.partial(mixer_a, p0=0, past=None, mk=mk_p[i], mv=mv_p[i],
                                      w_in=w_in_a[j], w_o=w_o[i])
            mix_s = functools.partial(mixer_a, p0=past_len,
                                      past=(cache_a_k[j], cache_a_v[j], cache_a_kidx[j]),
                                      mk=cache_mem_k[i], mv=cache_mem_v[i], w_in=w_in_a[j], w_o=w_o[i])
            xp, sp = macaron_layer(xp, mix_p, *ffn)
            xs, ss = macaron_layer(xs, mix_s, *ffn)
            st_a_p.append(sp)
            st_a_s.append(ss)
        else:
            mix_p = functools.partial(mixer_b, p0=0, past=None, mk=mk_p[i], mv=mv_p[i],
                                      w_in=w_in_b[j], w_o=w_o[i], rel_bias=rel_bias[j])
            mix_s = functools.partial(mixer_b, p0=past_len, past=(cache_b_k[j], cache_b_v[j]),
                                      mk=cache_mem_k[i], mv=cache_mem_v[i], w_in=w_in_b[j], w_o=w_o[i],
                                      rel_bias=rel_bias[j])
            xp, sp = macaron_layer(xp, mix_p, *ffn)
            xs, ss = macaron_layer(xs, mix_s, *ffn)
            st_b_p.append(sp)
            st_b_s.append(ss)
    new_a_k_prompt = jnp.stack([s[0] for s in st_a_p])
    new_a_v_prompt = jnp.stack([s[1] for s in st_a_p])
    new_a_kidx_prompt = jnp.stack([s[2] for s in st_a_p])
    new_b_k_prompt = jnp.stack([s[0] for s in st_b_p])
    new_b_v_prompt = jnp.stack([s[1] for s in st_b_p])
    new_a_k_sample = jnp.stack([s[0] for s in st_a_s])
    new_a_v_sample = jnp.stack([s[1] for s in st_a_s])
    new_a_kidx_sample = jnp.stack([s[2] for s in st_a_s])
    new_b_k_sample = jnp.stack([s[0] for s in st_b_s])
    new_b_v_sample = jnp.stack([s[1] for s in st_b_s])
    return (xp, xs,
            new_a_k_prompt, new_a_v_prompt, new_a_kidx_prompt,
            new_b_k_prompt, new_b_v_prompt, mk_p, mv_p,
            new_a_k_sample, new_a_v_sample, new_a_kidx_sample,
            new_b_k_sample, new_b_v_sample)
```

```python
import functools

import numpy as np
import jax
import jax.numpy as jnp
from jax import lax
from jax.experimental import pallas as pl
from jax.experimental.pallas import tpu as pltpu

CHUNK = 64
CHUNK_SHIFT = 6
N_HEADS = 12
N_KV_HEADS = 4
HEAD_DIM = 128
GROUP = N_HEADS // N_KV_HEADS
N_MEM_HEADS = 4
IDX_HEADS = 8
IDX_DIM = 64
TOPK_MAX = 256
WIN_B = 8 * CHUNK
REL_CLIP = 128
ROPE_THETA = 10000.0
LN_EPS = 1e-5
Q_W = N_HEADS * HEAD_DIM
KV_W = N_KV_HEADS * HEAD_DIM
MQ_W = N_MEM_HEADS * HEAD_DIM
IDXQ_W = IDX_HEADS * IDX_DIM
O_W = Q_W + MQ_W
PROJ_A = Q_W + 2 * KV_W + MQ_W + IDXQ_W + IDX_DIM + IDX_HEADS
PROJ_B = Q_W + 2 * KV_W + MQ_W
ATTN_SCALE = HEAD_DIM ** -0.5
IDX_W_SCALE = (IDX_HEADS * IDX_DIM) ** -0.5
NEG_INF = -1e30

LANES = 128
VMEM_LIMIT = 56 * 1024 * 1024

F32 = jnp.float32
BF16 = jnp.bfloat16
INT_MIN = -(2 ** 31)
NEG_INF_KEY = int(np.array(NEG_INF, np.float32).view(np.int32)) ^ 0x7FFFFFFF


def _dot(a, b):
    return jnp.dot(a, b, preferred_element_type=F32)


def _dot_nt(a, b):
    return lax.dot_general(a, b, (((1,), (1,)), ((), ())), preferred_element_type=F32)


def _params(n_axes):
    return pltpu.CompilerParams(dimension_semantics=("arbitrary",) * n_axes,
                                vmem_limit_bytes=VMEM_LIMIT)


def _resident(block_shape, index_map):
    return pl.BlockSpec(block_shape, index_map, pipeline_mode=pl.Buffered(1))


def _layer_norm(y, g, b):
    mu = jnp.mean(y, axis=-1, keepdims=True)
    d = y - mu
    var = jnp.mean(d * d, axis=-1, keepdims=True)
    return d * lax.rsqrt(var + LN_EPS) * g + b


def _ffn_ln_kernel(x_ref, wg_ref, wu_ref, wo_ref, g_ref, b_ref, o_ref, xb_ref, acc_ref, *, alpha):
    j = pl.program_id(1)

    @pl.when(j == 0)
    def _():
        xb_ref[...] = x_ref[...].astype(BF16)
        acc_ref[...] = jnp.zeros_like(acc_ref)

    xb = xb_ref[...]
    gate = _dot(xb, wg_ref[...])
    up = _dot(xb, wu_ref[...])
    h = (gate / (1.0 + jnp.exp(-gate))) * up
    acc_ref[...] += _dot(h.astype(BF16), wo_ref[...])

    @pl.when(j == pl.num_programs(1) - 1)
    def _():
        y = alpha * x_ref[...] + 0.5 * acc_ref[...]
        o_ref[...] = _layer_norm(y, g_ref[...], b_ref[...])


def _ffn_ln(x, w_in, w_out, g, b, alpha):
    n, d = x.shape
    dff = w_out.shape[0]
    tm = min(512, n)
    tf = 512 if dff % 512 == 0 else LANES
    nj = dff // tf
    return pl.pallas_call(
        functools.partial(_ffn_ln_kernel, alpha=alpha),
        grid=(n // tm, nj),
        in_specs=[
            pl.BlockSpec((tm, d), lambda i, j: (i, 0)),
            pl.BlockSpec((d, tf), lambda i, j: (0, j)),
            pl.BlockSpec((d, tf), lambda i, j: (0, j + nj)),
            pl.BlockSpec((tf, d), lambda i, j: (j, 0)),
            pl.BlockSpec((1, d), lambda i, j: (0, 0)),
            pl.BlockSpec((1, d), lambda i, j: (0, 0)),
        ],
        out_specs=pl.BlockSpec((tm, d), lambda i, j: (i, 0)),
        out_shape=jax.ShapeDtypeStruct((n, d), F32),
        scratch_shapes=[pltpu.VMEM((tm, d), BF16), pltpu.VMEM((tm, d), F32)],
        compiler_params=_params(2),
        name="ffn_ln",
    )(x, w_in, w_in, w_out, g.reshape(1, d), b.reshape(1, d))


def _oproj_ln_kernel(a_ref, w_ref, x_ref, g_ref, b_ref, o_ref, *, alpha):
    y = alpha * x_ref[...] + _dot(a_ref[...], w_ref[...])
    o_ref[...] = _layer_norm(y, g_ref[...], b_ref[...])


def _oproj_ln(a, w, x, g, b, alpha):
    n, d = x.shape
    ow = a.shape[1]
    tm = min(512, n)
    return pl.pallas_call(
        functools.partial(_oproj_ln_kernel, alpha=alpha),
        grid=(n // tm,),
        in_specs=[
            pl.BlockSpec((tm, ow), lambda i: (i, 0)),
            _resident((ow, d), lambda i: (0, 0)),
            pl.BlockSpec((tm, d), lambda i: (i, 0)),
            pl.BlockSpec((1, d), lambda i: (0, 0)),
            pl.BlockSpec((1, d), lambda i: (0, 0)),
        ],
        out_specs=pl.BlockSpec((tm, d), lambda i: (i, 0)),
        out_shape=jax.ShapeDtypeStruct((n, d), F32),
        compiler_params=_params(1),
        name="oproj_ln",
    )(a, w, x, g.reshape(1, d), b.reshape(1, d))


def _mem_kv_kernel(x_ref, w_ref, of_ref, ob_ref):
    y = _dot(x_ref[...].astype(BF16), w_ref[...])
    of_ref[...] = y
    ob_ref[...] = y.astype(BF16)


def _mem_kv(mem, w):
    n, d = mem.shape
    depth, _, c = w.shape
    return pl.pallas_call(
        _mem_kv_kernel,
        grid=(depth,),
        in_specs=[
            _resident((n, d), lambda l: (0, 0)),
            pl.BlockSpec((None, d, c), lambda l: (l, 0, 0)),
        ],
        out_specs=[
            pl.BlockSpec((None, n, c), lambda l: (l, 0, 0)),
            pl.BlockSpec((None, n, c), lambda l: (l, 0, 0)),
        ],
        out_shape=[jax.ShapeDtypeStruct((depth, n, c), F32),
                   jax.ShapeDtypeStruct((depth, n, c), BF16)],
        compiler_params=_params(1),
        name="mem_kv",
    )(mem, w)


def _rope_tables(pos):
    posf = pos.astype(F32)[:, None]

    def cs(half):
        inv_freq = ROPE_THETA ** (-jnp.arange(half, dtype=F32) / half)
        ang = posf * inv_freq[None, :]
        return jnp.cos(ang), jnp.sin(ang)

    c64, s64 = cs(HEAD_DIM // 2)
    c32, s32 = cs(IDX_DIM // 2)
    z32 = jnp.zeros_like(s32)
    cos128 = jnp.concatenate([c64, c64], -1)
    sin128 = jnp.concatenate([-s64, s64], -1)
    cos64 = jnp.concatenate([c32] * 4, -1)
    sa64 = jnp.concatenate([-s32, z32, -s32, z32], -1)
    sb64 = jnp.concatenate([z32, s32, z32, s32], -1)
    lane = jnp.arange(LANES)[None, :]
    first = lane < IDX_DIM
    return jnp.stack([cos128, sin128, cos64, sa64, sb64,
                      jnp.where(first, cos64, 1.0), jnp.where(first, sa64, 0.0),
                      jnp.where(first, sb64, 0.0)])


def _proj_kernel(x_ref, w_ref, *refs, layer_a):
    xb = x_ref[...].astype(BF16)
    if layer_a:
        tab_ref, q_ref, kf_ref, vf_ref, kb_ref, vb_ref, qm_ref, qi_ref, tail_ref, kid_ref = refs
    else:
        q_ref, kf_ref, vf_ref, kb_ref, vb_ref, qm_ref = refs

    def rope128(y):
        return y * tab_ref[0] + pltpu.roll(y, HEAD_DIM // 2, axis=1) * tab_ref[1]

    def rope64(y, t0):
        return (y * tab_ref[t0] + pltpu.roll(y, LANES - IDX_DIM // 2, axis=1) * tab_ref[t0 + 1]
                + pltpu.roll(y, IDX_DIM // 2, axis=1) * tab_ref[t0 + 2])

    def section(col0, width, fn):
        step = 512
        for c0 in range(0, width, step):
            cw = min(step, width - c0)
            y = _dot(xb, w_ref[:, col0 + c0:col0 + c0 + cw])
            for s in range(cw // LANES):
                fn(c0 + s * LANES, y[:, s * LANES:(s + 1) * LANES])

    def put(ref, fn=None):
        def f(off, y):
            ref[:, off:off + LANES] = (y if fn is None else fn(y)).astype(ref.dtype)
        return f

    def put2(ref_f, ref_b, fn=None):
        def f(off, y):
            y = y if fn is None else fn(y)
            ref_f[:, off:off + LANES] = y
            ref_b[:, off:off + LANES] = y.astype(BF16)
        return f

    rq = rope128 if layer_a else None
    section(0, Q_W, put(q_ref, rq))
    section(Q_W, KV_W, put2(kf_ref, kb_ref, rq))
    section(Q_W + KV_W, KV_W, put2(vf_ref, vb_ref))
    section(Q_W + 2 * KV_W, MQ_W, put(qm_ref))
    if layer_a:
        c = Q_W + 2 * KV_W + MQ_W
        section(c, IDXQ_W, put(qi_ref, lambda y: rope64(y, 2)))

        def tail(off, y):
            t = rope64(y, 5)
            tail_ref[...] = t
            lane = lax.broadcasted_iota(jnp.int32, t.shape, 1)
            dup = jnp.where(lane < IDX_DIM, t, pltpu.roll(t, IDX_DIM, axis=1))
            kid_ref[...] = dup.astype(BF16)
        section(c + IDXQ_W, LANES, tail)


def _proj(x, w, tab, layer_a):
    n, d = x.shape
    pw = w.shape[1]
    tm = min(512, n)
    row = lambda i: (i, 0)
    in_specs = [pl.BlockSpec((tm, d), row), _resident((d, pw), lambda i: (0, 0))]
    args = [x, w]
    outs = [(Q_W, BF16), (KV_W, F32), (KV_W, F32), (KV_W, BF16), (KV_W, BF16), (MQ_W, BF16)]
    if layer_a:
        nt = tab.shape[1] // tm
        in_specs.append(pl.BlockSpec((8, tm, LANES), lambda i: (0, i % nt, 0)))
        args.append(tab)
        outs += [(IDXQ_W, BF16), (LANES, F32), (LANES, BF16)]
    return pl.pallas_call(
        functools.partial(_proj_kernel, layer_a=layer_a),
        grid=(n // tm,),
        in_specs=in_specs,
        out_specs=[pl.BlockSpec((tm, wd), row) for wd, _ in outs],
        out_shape=[jax.ShapeDtypeStruct((n, wd), dt) for wd, dt in outs],
        compiler_params=_params(1),
        name="proj_a" if layer_a else "proj_b",
    )(*args)


def _stack_heads(q_ref, g):
    return jnp.concatenate(
        [q_ref[:, (GROUP * g + r) * HEAD_DIM:(GROUP * g + r + 1) * HEAD_DIM] for r in range(GROUP)],
        axis=0)


def _unstack_heads(o_ref, g, o, tq):
    for r in range(GROUP):
        h = GROUP * g + r
        o_ref[:, h * HEAD_DIM:(h + 1) * HEAD_DIM] = o[r * tq:(r + 1) * tq].astype(o_ref.dtype)


def _mem_attend(qm_ref, mk_ref, mv_ref, o_ref):
    for hm in range(N_MEM_HEADS):
        sl = slice(hm * HEAD_DIM, (hm + 1) * HEAD_DIM)
        s = _dot_nt(qm_ref[:, sl], mk_ref[:, sl]) * ATTN_SCALE
        m = jnp.max(s, axis=1, keepdims=True)
        p = jnp.exp(s - m)
        l = jnp.sum(p, axis=1, keepdims=True)
        om = _dot(p.astype(BF16), mv_ref[:, sl]) / l
        o_ref[:, Q_W + hm * HEAD_DIM:Q_W + (hm + 1) * HEAD_DIM] = om.astype(o_ref.dtype)


def _sort_key(x):
    bits = lax.bitcast_convert_type(x, jnp.int32)
    return bits ^ ((bits >> 31) & 0x7FFFFFFF)


def _dsa_kernel(q_ref, qm_ref, qi_ref, tail_ref, k_ref, v_ref, kid_ref, mk_ref, mv_ref,
                o_ref, keys_ref, *, tq, tk, s_valid, pos0, topk):
    i = pl.program_id(1)
    row0 = pos0 + i * tq
    rpos = row0 + lax.broadcasted_iota(jnp.int32, (tq, 1), 0)
    lim = jnp.minimum(((rpos >> CHUNK_SHIFT) + 1) * CHUNK, s_valid)
    lim_max = jnp.minimum((((row0 + tq - 1) >> CHUNK_SHIFT) + 1) * CHUNK, s_valid)
    n_kt = (lim_max + tk - 1) // tk
    reps = tk // LANES

    lane = lax.broadcasted_iota(jnp.int32, (tq, LANES), 1)
    qh = []
    for p in range(IDX_HEADS // 2):
        qp = qi_ref[:, p * LANES:(p + 1) * LANES]
        zero = jnp.zeros_like(qp)
        qh.append(jnp.where(lane < IDX_DIM, qp, zero))
        qh.append(jnp.where(lane >= IDX_DIM, qp, zero))
    wi = tail_ref[:, IDX_DIM:IDX_DIM + IDX_HEADS] * IDX_W_SCALE
    wb = [jnp.broadcast_to(wi[:, h:h + 1], (tq, tk)) for h in range(IDX_HEADS)]
    col = lax.broadcasted_iota(jnp.int32, (tq, tk), 1)

    def idx_body(kt, carry):
        kid = kid_ref[pl.ds(pl.multiple_of(kt * tk, tk), tk), :]
        acc = jnp.zeros((tq, tk), F32)
        for h in range(IDX_HEADS):
            acc = acc + wb[h] * jnp.maximum(_dot_nt(qh[h], kid), 0.0)
        score = jnp.where(col + kt * tk < lim, acc, NEG_INF)
        keys_ref[kt] = _sort_key(score)
        return carry

    lax.fori_loop(0, n_kt, idx_body, 0)

    def count_ge(cand):
        def body(kt, cnt):
            kk = keys_ref[kt]
            for c in range(reps):
                cnt = cnt + (kk[:, c * LANES:(c + 1) * LANES] >= cand).astype(jnp.int32)
            return cnt
        cnt = lax.fori_loop(0, n_kt, body, jnp.zeros((tq, LANES), jnp.int32))
        return jnp.sum(cnt.astype(F32), axis=1, keepdims=True)

    kf = float(topk)
    zeros = jnp.zeros((tq, LANES), jnp.int32)
    thr0 = jnp.where(count_ge(zeros) >= kf, zeros, jnp.full_like(zeros, INT_MIN))

    def search(it, thr):
        cand = thr | lax.shift_left(jnp.int32(1), 30 - it)
        return jnp.where(count_ge(cand) >= kf, cand, thr)

    thr = lax.fori_loop(0, 31, search, thr0)
    thr = jnp.maximum(thr, NEG_INF_KEY + 1)
    thr_t = jnp.concatenate([thr] * reps, axis=1)

    for g in range(N_KV_HEADS):
        q3 = _stack_heads(q_ref, g)
        hs = slice(g * HEAD_DIM, (g + 1) * HEAD_DIM)

        def att_body(kt, carry, q3=q3, hs=hs):
            m, l, acc = carry
            off = pl.multiple_of(kt * tk, tk)
            sel = keys_ref[kt] >= thr_t
            s = _dot_nt(q3, k_ref[pl.ds(off, tk), hs]) * ATTN_SCALE
            s = jnp.concatenate(
                [jnp.where(sel, s[r * tq:(r + 1) * tq], NEG_INF) for r in range(GROUP)], axis=0)
            m_new = jnp.maximum(m, jnp.max(s, axis=1, keepdims=True))
            a = jnp.exp(m - m_new)
            p = jnp.exp(s - m_new)
            l = a * l + jnp.sum(p, axis=1, keepdims=True)
            acc = a * acc + _dot(p.astype(BF16), v_ref[pl.ds(off, tk), hs])
            return m_new, l, acc

        init = (jnp.full((GROUP * tq, 1), NEG_INF, F32), jnp.zeros((GROUP * tq, 1), F32),
                jnp.zeros((GROUP * tq, HEAD_DIM), F32))
        m, l, acc = lax.fori_loop(0, n_kt, att_body, init)
        _unstack_heads(o_ref, g, acc / l, tq)

    _mem_attend(qm_ref, mk_ref, mv_ref, o_ref)


def _dsa(q, qm, qi, tail, k, v, kid, mk, mv, *, tq, tk, s_valid, pos0, topk):
    n = q.shape[0]
    nb, s_pad, _ = k.shape
    nq = n // nb // tq
    n_mem = mk.shape[0] // nb
    row = lambda b, i: (b * nq + i, 0)
    per_b = lambda b, i: (b, 0, 0)
    return pl.pallas_call(
        functools.partial(_dsa_kernel, tq=tq, tk=tk, s_valid=s_valid, pos0=pos0, topk=topk),
        grid=(nb, nq),
        in_specs=[
            pl.BlockSpec((tq, Q_W), row),
            pl.BlockSpec((tq, MQ_W), row),
            pl.BlockSpec((tq, IDXQ_W), row),
            pl.BlockSpec((tq, LANES), row),
            _resident((None, s_pad, KV_W), per_b),
            _resident((None, s_pad, KV_W), per_b),
            _resident((None, s_pad, LANES), per_b),
            pl.BlockSpec((n_mem, MQ_W), lambda b, i: (b, 0)),
            pl.BlockSpec((n_mem, MQ_W), lambda b, i: (b, 0)),
        ],
        out_specs=pl.BlockSpec((tq, O_W), row),
        out_shape=jax.ShapeDtypeStruct((n, O_W), BF16),
        scratch_shapes=[pltpu.VMEM((s_pad // tk, tq, tk), jnp.int32)],
        compiler_params=_params(2),
        name="dsa_attend",
    )(q, qm, qi, tail, k, v, kid, mk, mv)


def _band_table_kernel(rb_ref, o_ref, *, qc, band, prefix):
    h = pl.program_id(0)
    shape = o_ref.shape
    r = lax.broadcasted_iota(jnp.int32, shape, 0)
    w = lax.broadcasted_iota(jnp.int32, shape, 1)
    idx = jnp.clip(r - w + prefix, -REL_CLIP, REL_CLIP) + REL_CLIP
    lo = (r // qc) * qc
    inband = jnp.logical_and(w >= lo, w < lo + band)

    def body(t, acc):
        return jnp.where(idx == t, rb_ref[h, t], acc)

    acc = lax.fori_loop(0, 2 * REL_CLIP + 1, body, jnp.zeros(shape, F32))
    o_ref[...] = jnp.where(inband, acc, NEG_INF)


def _band_table(rel_bias, rows, width, qc, band, prefix):
    nh = rel_bias.shape[0]
    return pl.pallas_call(
        functools.partial(_band_table_kernel, qc=qc, band=band, prefix=prefix),
        grid=(nh,),
        in_specs=[pl.BlockSpec(memory_space=pltpu.SMEM)],
        out_specs=pl.BlockSpec((None, rows, width), lambda h: (h, 0, 0)),
        out_shape=jax.ShapeDtypeStruct((nh, rows, width), F32),
        compiler_params=_params(1),
        name="band_table",
    )(rel_bias)


def _band_kernel(q_ref, qm_ref, k_ref, v_ref, tab_ref, mk_ref, mv_ref, o_ref, *, tq, width, kp0):
    i = pl.program_id(1)
    off = pl.multiple_of(i * tq, tq)
    kp_ok = (kp0 + i * tq + lax.broadcasted_iota(jnp.int32, (GROUP * tq, width), 1)) >= 0
    for g in range(N_KV_HEADS):
        hs = slice(g * HEAD_DIM, (g + 1) * HEAD_DIM)
        q3 = _stack_heads(q_ref, g)
        s = _dot_nt(q3, k_ref[pl.ds(off, width), hs]) * ATTN_SCALE
        s = s + jnp.concatenate([tab_ref[GROUP * g + r] for r in range(GROUP)], axis=0)
        s = jnp.where(kp_ok, s, NEG_INF)
        m = jnp.max(s, axis=1, keepdims=True)
        p = jnp.exp(s - m)
        l = jnp.sum(p, axis=1, keepdims=True)
        o = _dot(p.astype(BF16), v_ref[pl.ds(off, width), hs]) / l
        _unstack_heads(o_ref, g, o, tq)
    _mem_attend(qm_ref, mk_ref, mv_ref, o_ref)


def _band(q, qm, k_all, v_all, table, mk, mv, *, tq, kp0):
    n = q.shape[0]
    nb, l_pad, _ = k_all.shape
    nq = n // nb // tq
    n_mem = mk.shape[0] // nb
    width = table.shape[2]
    row = lambda b, i: (b * nq + i, 0)
    per_b = lambda b, i: (b, 0, 0)
    return pl.pallas_call(
        functools.partial(_band_kernel, tq=tq, width=width, kp0=kp0),
        grid=(nb, nq),
        in_specs=[
            pl.BlockSpec((tq, Q_W), row),
            pl.BlockSpec((tq, MQ_W), row),
            _resident((None, l_pad, KV_W), per_b),
            _resident((None, l_pad, KV_W), per_b),
            _resident(table.shape, lambda b, i: (0, 0, 0)),
            pl.BlockSpec((n_mem, MQ_W), lambda b, i: (b, 0)),
            pl.BlockSpec((n_mem, MQ_W), lambda b, i: (b, 0)),
        ],
        out_specs=pl.BlockSpec((tq, O_W), row),
        out_shape=jax.ShapeDtypeStruct((n, O_W), BF16),
        compiler_params=_params(2),
        name="band_attend",
    )(q, qm, k_all, v_all, table, mk, mv)


def _pad_rows(a, rows):
    return jnp.pad(a, ((0, 0), (0, rows - a.shape[1]), (0, 0)))


def kernel(x_prompt, x_sample, mem_prompt, cache_a_k, cache_a_v, cache_a_kidx, cache_b_k, cache_b_v,
           cache_mem_k, cache_mem_v, w_in_a, w_in_b, w_o, rel_bias, w_mem_kv,
           w_ffn1_in, w_ffn1_out, w_ffn2_in, w_ffn2_out, ln_g, ln_b):
    depth = w_o.shape[0]
    alpha = (2 * depth) ** 0.25
    bp, tp, d = x_prompt.shape
    bs, ts, _ = x_sample.shape
    past = cache_a_k.shape[2]
    n_mem = mem_prompt.shape[1]
    assert tp % LANES == 0 and ts % 16 == 0 and ts <= CHUNK
    assert cache_b_k.shape[2] == WIN_B and tp >= WIN_B

    pa_pad = -(-PROJ_A // LANES) * LANES
    wa = jnp.pad(w_in_a, ((0, 0), (0, 0), (0, pa_pad - PROJ_A))).astype(BF16)
    wb = w_in_b.astype(BF16)
    wo = w_o.astype(BF16)
    w1i, w1o = w_ffn1_in.astype(BF16), w_ffn1_out.astype(BF16)
    w2i, w2o = w_ffn2_in.astype(BF16), w_ffn2_out.astype(BF16)

    memf, memb = _mem_kv(mem_prompt.reshape(bp * n_mem, d), w_mem_kv.astype(BF16))
    mk_p = memf[..., :MQ_W].reshape(depth, bp, n_mem, N_MEM_HEADS, HEAD_DIM)
    mv_p = memf[..., MQ_W:].reshape(depth, bp, n_mem, N_MEM_HEADS, HEAD_DIM)
    mkb_p, mvb_p = memb[..., :MQ_W], memb[..., MQ_W:]
    mkb_s = cache_mem_k.astype(BF16).reshape(depth, bs * n_mem, MQ_W)
    mvb_s = cache_mem_v.astype(BF16).reshape(depth, bs * n_mem, MQ_W)

    tab_p = _rope_tables(jnp.arange(tp))
    tab_s = jnp.tile(_rope_tables(past + jnp.arange(ts)), (1, bs, 1))

    xp = x_prompt.reshape(bp * tp, d)
    xs = x_sample.reshape(bs * ts, d)
    st = {k: [] for k in ("akp", "avp", "aip", "bkp", "bvp", "aks", "avs", "ais", "bks", "bvs")}
    heads = lambda a, b, t: a.reshape(b, t, N_KV_HEADS, HEAD_DIM)

    for i in range(depth):
        j = i // 2
        g, b = ln_g[i], ln_b[i]
        xp = _ffn_ln(xp, w1i[i], w1o[i], g[0], b[0], alpha)
        xs = _ffn_ln(xs, w1i[i], w1o[i], g[0], b[0], alpha)
        if i % 2 == 0:
            q, kf, vf, kb, vb, qm, qi, tail, kid = _proj(xp, wa[j], tab_p, True)
            tq = LANES
            ap = _dsa(q, qm, qi, tail, kb.reshape(bp, tp, KV_W), vb.reshape(bp, tp, KV_W),
                      kid.reshape(bp, tp, LANES), mkb_p[i], mvb_p[i],
                      tq=tq, tk=2 * LANES, s_valid=tp, pos0=0, topk=min(TOPK_MAX, tp // 4))
            st["akp"].append(heads(kf, bp, tp))
            st["avp"].append(heads(vf, bp, tp))
            st["aip"].append(tail[:, :IDX_DIM].reshape(bp, tp, IDX_DIM))

            q, kf, vf, kb, vb, qm, qi, tail, kid = _proj(xs, wa[j], tab_s, True)
            tk = 2 * LANES
            s_valid = past + ts
            s_pad = -(-s_valid // tk) * tk
            ci = cache_a_kidx[j].astype(BF16)
            k_all = _pad_rows(jnp.concatenate(
                [cache_a_k[j].astype(BF16).reshape(bs, past, KV_W), kb.reshape(bs, ts, KV_W)], 1), s_pad)
            v_all = _pad_rows(jnp.concatenate(
                [cache_a_v[j].astype(BF16).reshape(bs, past, KV_W), vb.reshape(bs, ts, KV_W)], 1), s_pad)
            kid_all = _pad_rows(jnp.concatenate(
                [jnp.concatenate([ci, ci], -1), kid.reshape(bs, ts, LANES)], 1), s_pad)
            as_ = _dsa(q, qm, qi, tail, k_all, v_all, kid_all, mkb_s[i], mvb_s[i],
                       tq=ts, tk=tk, s_valid=s_valid, pos0=past, topk=min(TOPK_MAX, s_valid // 4))
            st["aks"].append(heads(kf, bs, ts))
            st["avs"].append(heads(vf, bs, ts))
            st["ais"].append(tail[:, :IDX_DIM].reshape(bs, ts, IDX_DIM))
        else:
            q, kf, vf, kb, vb, qm = _proj(xp, wb[j], None, False)
            tq = LANES
            table = _band_table(rel_bias[j], tq, tq + WIN_B, CHUNK, WIN_B + CHUNK, WIN_B)
            zpad = jnp.zeros((bp, WIN_B, KV_W), BF16)
            ap = _band(q, qm, jnp.concatenate([zpad, kb.reshape(bp, tp, KV_W)], 1),
                       jnp.concatenate([zpad, vb.reshape(bp, tp, KV_W)], 1),
                       table, mkb_p[i], mvb_p[i], tq=tq, kp0=-WIN_B)
            n_keep = min(WIN_B, tp)
            st["bkp"].append(heads(kf, bp, tp)[:, tp - n_keep:])
            st["bvp"].append(heads(vf, bp, tp)[:, tp - n_keep:])

            q, kf, vf, kb, vb, qm = _proj(xs, wb[j], None, False)
            width = -(-(WIN_B + ts) // LANES) * LANES
            table = _band_table(rel_bias[j], ts, width, ts, WIN_B + ts, WIN_B)
            k_all = _pad_rows(jnp.concatenate(
                [cache_b_k[j].astype(BF16).reshape(bs, WIN_B, KV_W), kb.reshape(bs, ts, KV_W)], 1), width)
            v_all = _pad_rows(jnp.concatenate(
                [cache_b_v[j].astype(BF16).reshape(bs, WIN_B, KV_W), vb.reshape(bs, ts, KV_W)], 1), width)
            as_ = _band(q, qm, k_all, v_all, table, mkb_s[i], mvb_s[i], tq=ts, kp0=past - WIN_B)
            st["bks"].append(jnp.concatenate([cache_b_k[j][:, ts:], heads(kf, bs, ts)], 1))
            st["bvs"].append(jnp.concatenate([cache_b_v[j][:, ts:], heads(vf, bs, ts)], 1))

        xp2 = _oproj_ln(ap, wo[i], xp, g[1], b[1], alpha)
        xs2 = _oproj_ln(as_, wo[i], xs, g[1], b[1], alpha)
        xp = _ffn_ln(xp2, w2i[i], w2o[i], g[2], b[2], alpha)
        xs = _ffn_ln(xs2, w2i[i], w2o[i], g[2], b[2], alpha)

    stack = lambda name: jnp.stack(st[name])
    return (xp.reshape(bp, tp, d), xs.reshape(bs, ts, d),
            stack("akp"), stack("avp"), stack("aip"), stack("bkp"), stack("bvp"), mk_p, mv_p,
            stack("aks"), stack("avs"), stack("ais"), stack("bks"), stack("bvs"))
```

```python
import functools

import numpy as np
import jax
import jax.numpy as jnp
from jax import lax
from jax.experimental import pallas as pl
from jax.experimental.pallas import tpu as pltpu

CHUNK = 64
CHUNK_SHIFT = 6
N_HEADS = 12
N_KV_HEADS = 4
HEAD_DIM = 128
GROUP = N_HEADS // N_KV_HEADS
N_MEM_HEADS = 4
IDX_HEADS = 8
IDX_DIM = 64
TOPK_MAX = 256
WIN_B = 8 * CHUNK
REL_CLIP = 128
ROPE_THETA = 10000.0
LN_EPS = 1e-5
Q_W = N_HEADS * HEAD_DIM
KV_W = N_KV_HEADS * HEAD_DIM
MQ_W = N_MEM_HEADS * HEAD_DIM
IDXQ_W = IDX_HEADS * IDX_DIM
O_W = Q_W + MQ_W
PROJ_A = Q_W + 2 * KV_W + MQ_W + IDXQ_W + IDX_DIM + IDX_HEADS
PROJ_B = Q_W + 2 * KV_W + MQ_W
ATTN_SCALE = HEAD_DIM ** -0.5
IDX_W_SCALE = (IDX_HEADS * IDX_DIM) ** -0.5
NEG_INF = -1e30
EXP2_SCALE = ATTN_SCALE * float(np.log2(np.e))

LANES = 128
TQ = LANES
DSA_TK = 4 * LANES
V_ONES = 16
VMEM_LIMIT = 56 * 1024 * 1024

F32 = jnp.float32
BF16 = jnp.bfloat16
INT_MIN = -(2 ** 31)
NEG_INF_KEY = int(np.array(NEG_INF, np.float32).view(np.int32)) ^ 0x7FFFFFFF


def _dot(a, b):
    return jnp.dot(a, b, preferred_element_type=F32)


def _dot_nt(a, b):
    return lax.dot_general(a, b, (((1,), (1,)), ((), ())), preferred_element_type=F32)


def _params(n_axes):
    return pltpu.CompilerParams(dimension_semantics=("arbitrary",) * n_axes,
                                vmem_limit_bytes=VMEM_LIMIT)


def _resident(block_shape, index_map):
    return pl.BlockSpec(block_shape, index_map, pipeline_mode=pl.Buffered(1))


def _layer_norm(y, g, b):
    mu = jnp.mean(y, axis=-1, keepdims=True)
    d = y - mu
    var = jnp.mean(d * d, axis=-1, keepdims=True)
    return d * lax.rsqrt(var + LN_EPS) * g + b


def _ffn_ln_kernel(x_ref, wg_ref, wu_ref, wo_ref, g_ref, b_ref, o_ref, xb_ref, acc_ref, *, alpha):
    j = pl.program_id(1)

    @pl.when(j == 0)
    def _():
        xb_ref[...] = x_ref[...].astype(BF16)
        acc_ref[...] = jnp.zeros_like(acc_ref)

    xb = xb_ref[...]
    gate = _dot(xb, wg_ref[...])
    up = _dot(xb, wu_ref[...])
    h = (gate / (1.0 + jnp.exp(-gate))) * up
    acc_ref[...] += _dot(h.astype(BF16), wo_ref[...])

    @pl.when(j == pl.num_programs(1) - 1)
    def _():
        y = alpha * x_ref[...] + 0.5 * acc_ref[...]
        o_ref[...] = _layer_norm(y, g_ref[...], b_ref[...])


def _ffn_ln(x, w_in, w_out, g, b, alpha):
    n, d = x.shape
    dff = w_out.shape[0]
    tm = min(512, n)
    tf = 512 if dff % 512 == 0 else LANES
    nj = dff // tf
    return pl.pallas_call(
        functools.partial(_ffn_ln_kernel, alpha=alpha),
        grid=(n // tm, nj),
        in_specs=[
            pl.BlockSpec((tm, d), lambda i, j: (i, 0)),
            pl.BlockSpec((d, tf), lambda i, j: (0, j)),
            pl.BlockSpec((d, tf), lambda i, j: (0, j + nj)),
            pl.BlockSpec((tf, d), lambda i, j: (j, 0)),
            pl.BlockSpec((1, d), lambda i, j: (0, 0)),
            pl.BlockSpec((1, d), lambda i, j: (0, 0)),
        ],
        out_specs=pl.BlockSpec((tm, d), lambda i, j: (i, 0)),
        out_shape=jax.ShapeDtypeStruct((n, d), F32),
        scratch_shapes=[pltpu.VMEM((tm, d), BF16), pltpu.VMEM((tm, d), F32)],
        compiler_params=_params(2),
        name="ffn_ln",
    )(x, w_in, w_in, w_out, g.reshape(1, d), b.reshape(1, d))


def _oproj_ln_kernel(a_ref, w_ref, x_ref, g_ref, b_ref, o_ref, *, alpha):
    y = alpha * x_ref[...] + _dot(a_ref[...], w_ref[...])
    o_ref[...] = _layer_norm(y, g_ref[...], b_ref[...])


def _oproj_ln(a, w, x, g, b, alpha):
    n, d = x.shape
    ow = a.shape[1]
    tm = min(512, n)
    return pl.pallas_call(
        functools.partial(_oproj_ln_kernel, alpha=alpha),
        grid=(n // tm,),
        in_specs=[
            pl.BlockSpec((tm, ow), lambda i: (i, 0)),
            _resident((ow, d), lambda i: (0, 0)),
            pl.BlockSpec((tm, d), lambda i: (i, 0)),
            pl.BlockSpec((1, d), lambda i: (0, 0)),
            pl.BlockSpec((1, d), lambda i: (0, 0)),
        ],
        out_specs=pl.BlockSpec((tm, d), lambda i: (i, 0)),
        out_shape=jax.ShapeDtypeStruct((n, d), F32),
        compiler_params=_params(1),
        name="oproj_ln",
    )(a, w, x, g.reshape(1, d), b.reshape(1, d))


def _mem_kv_kernel(x_ref, w_ref, of_ref, ob_ref):
    y = _dot(x_ref[...].astype(BF16), w_ref[...])
    of_ref[...] = y
    ob_ref[...] = y.astype(BF16)


def _mem_kv(mem, w):
    n, d = mem.shape
    depth, _, c = w.shape
    return pl.pallas_call(
        _mem_kv_kernel,
        grid=(depth,),
        in_specs=[
            _resident((n, d), lambda l: (0, 0)),
            pl.BlockSpec((None, d, c), lambda l: (l, 0, 0)),
        ],
        out_specs=[
            pl.BlockSpec((None, n, c), lambda l: (l, 0, 0)),
            pl.BlockSpec((None, n, c), lambda l: (l, 0, 0)),
        ],
        out_shape=[jax.ShapeDtypeStruct((depth, n, c), F32),
                   jax.ShapeDtypeStruct((depth, n, c), BF16)],
        compiler_params=_params(1),
        name="mem_kv",
    )(mem, w)


def _rope_tables(pos):
    posf = pos.astype(F32)[:, None]

    def cs(half):
        inv_freq = ROPE_THETA ** (-jnp.arange(half, dtype=F32) / half)
        ang = posf * inv_freq[None, :]
        return jnp.cos(ang), jnp.sin(ang)

    c64, s64 = cs(HEAD_DIM // 2)
    c32, s32 = cs(IDX_DIM // 2)
    z32 = jnp.zeros_like(s32)
    cos128 = jnp.concatenate([c64, c64], -1)
    sin128 = jnp.concatenate([-s64, s64], -1)
    cos64 = jnp.concatenate([c32] * 4, -1)
    sa64 = jnp.concatenate([-s32, z32, -s32, z32], -1)
    sb64 = jnp.concatenate([z32, s32, z32, s32], -1)
    lane = jnp.arange(LANES)[None, :]
    first = lane < IDX_DIM
    return jnp.stack([cos128, sin128, cos64, sa64, sb64,
                      jnp.where(first, cos64, 1.0), jnp.where(first, sa64, 0.0),
                      jnp.where(first, sb64, 0.0)])


def _proj_kernel(x_ref, w_ref, *refs, layer_a):
    xb = x_ref[...].astype(BF16)
    if layer_a:
        tab_ref, q_ref, kf_ref, vf_ref, kb_ref, vb_ref, qm_ref, qi_ref, tail_ref, kid_ref = refs
    else:
        q_ref, kf_ref, vf_ref, kb_ref, vb_ref, qm_ref = refs

    def rope128(y):
        return y * tab_ref[0] + pltpu.roll(y, HEAD_DIM // 2, axis=1) * tab_ref[1]

    def rope64(y, t0):
        return (y * tab_ref[t0] + pltpu.roll(y, LANES - IDX_DIM // 2, axis=1) * tab_ref[t0 + 1]
                + pltpu.roll(y, IDX_DIM // 2, axis=1) * tab_ref[t0 + 2])

    def section(col0, width, fn):
        step = 512
        for c0 in range(0, width, step):
            cw = min(step, width - c0)
            y = _dot(xb, w_ref[:, col0 + c0:col0 + c0 + cw])
            for s in range(cw // LANES):
                fn(c0 + s * LANES, y[:, s * LANES:(s + 1) * LANES])

    def put(ref, fn=None):
        def f(off, y):
            ref[:, off:off + LANES] = (y if fn is None else fn(y)).astype(ref.dtype)
        return f

    def put2(ref_f, ref_b, fn=None):
        def f(off, y):
            y = y if fn is None else fn(y)
            ref_f[:, off:off + LANES] = y
            ref_b[:, off:off + LANES] = y.astype(BF16)
        return f

    rq = rope128 if layer_a else None
    section(0, Q_W, put(q_ref, rq))
    section(Q_W, KV_W, put2(kf_ref, kb_ref, rq))
    section(Q_W + KV_W, KV_W, put2(vf_ref, vb_ref))
    section(Q_W + 2 * KV_W, MQ_W, put(qm_ref))
    if layer_a:
        c = Q_W + 2 * KV_W + MQ_W
        section(c, IDXQ_W, put(qi_ref, lambda y: rope64(y, 2)))

        def tail(off, y):
            t = rope64(y, 5)
            tail_ref[...] = t
            lane = lax.broadcasted_iota(jnp.int32, t.shape, 1)
            dup = jnp.where(lane < IDX_DIM, t, pltpu.roll(t, IDX_DIM, axis=1))
            kid_ref[...] = dup.astype(BF16)
        section(c + IDXQ_W, LANES, tail)


def _proj(x, w, tab, layer_a):
    n, d = x.shape
    pw = w.shape[1]
    tm = min(512, n)
    row = lambda i: (i, 0)
    in_specs = [pl.BlockSpec((tm, d), row), _resident((d, pw), lambda i: (0, 0))]
    args = [x, w]
    outs = [(Q_W, BF16), (KV_W, F32), (KV_W, F32), (KV_W, BF16), (KV_W, BF16), (MQ_W, BF16)]
    if layer_a:
        nt = tab.shape[1] // tm
        in_specs.append(pl.BlockSpec((8, tm, LANES), lambda i: (0, i % nt, 0)))
        args.append(tab)
        outs += [(IDXQ_W, BF16), (LANES, F32), (LANES, BF16)]
    return pl.pallas_call(
        functools.partial(_proj_kernel, layer_a=layer_a),
        grid=(n // tm,),
        in_specs=in_specs,
        out_specs=[pl.BlockSpec((tm, wd), row) for wd, _ in outs],
        out_shape=[jax.ShapeDtypeStruct((n, wd), dt) for wd, dt in outs],
        compiler_params=_params(1),
        name="proj_a" if layer_a else "proj_b",
    )(*args)


def _stack_heads(q_ref, g):
    return jnp.concatenate(
        [q_ref[:, (GROUP * g + r) * HEAD_DIM:(GROUP * g + r + 1) * HEAD_DIM] for r in range(GROUP)],
        axis=0)


def _unstack_heads(o_ref, g, o, tq):
    for r in range(GROUP):
        h = GROUP * g + r
        o_ref[:, h * HEAD_DIM:(h + 1) * HEAD_DIM] = o[r * tq:(r + 1) * tq].astype(o_ref.dtype)


def _mem_attend(qm_ref, mk_ref, mv_ref, o_ref):
    for hm in range(N_MEM_HEADS):
        sl = slice(hm * HEAD_DIM, (hm + 1) * HEAD_DIM)
        s = _dot_nt(qm_ref[:, sl], mk_ref[:, sl]) * ATTN_SCALE
        m = jnp.max(s, axis=1, keepdims=True)
        p = jnp.exp(s - m)
        l = jnp.sum(p, axis=1, keepdims=True)
        om = _dot(p.astype(BF16), mv_ref[:, sl]) / l
        o_ref[:, Q_W + hm * HEAD_DIM:Q_W + (hm + 1) * HEAD_DIM] = om.astype(o_ref.dtype)


def _sort_key(x):
    bits = lax.bitcast_convert_type(x, jnp.int32)
    return bits ^ ((bits >> 31) & 0x7FFFFFFF)


def _dsa_kernel(q_ref, qm_ref, qi_ref, tail_ref, k_ref, vt_ref, kid_ref, mk_ref, mv_ref,
                o_ref, keys_ref, hi_ref, lo_ref, sa_ref, sb_ref, *, tk, s_valid, pos0, topk):
    tq = TQ
    i = pl.program_id(1)
    row0 = pos0 + i * tq
    qpos = row0 + lax.broadcasted_iota(jnp.int32, (1, tq), 1)
    lim = jnp.minimum(((qpos >> CHUNK_SHIFT) + 1) * CHUNK, s_valid)
    lim_max = jnp.minimum((((row0 + tq - 1) >> CHUNK_SHIFT) + 1) * CHUNK, s_valid)
    n_kt = (lim_max + tk - 1) // tk

    lane = lax.broadcasted_iota(jnp.int32, (tq, LANES), 1)
    pairs = []
    for p in range(IDX_HEADS // 2):
        qp = qi_ref[:, p * LANES:(p + 1) * LANES]
        zero = jnp.zeros_like(qp)
        pairs.append(jnp.concatenate(
            [jnp.where(lane < IDX_DIM, qp, zero), jnp.where(lane >= IDX_DIM, qp, zero)], axis=0))
    wt = tail_ref[...].T[IDX_DIM:IDX_DIM + IDX_HEADS, :] * IDX_W_SCALE
    krow = lax.broadcasted_iota(jnp.int32, (tk, tq), 0)

    def idx_body(kt, carry):
        kid = kid_ref[pl.ds(pl.multiple_of(kt * tk, tk), tk), :]
        acc = jnp.zeros((tk, tq), F32)
        for p in range(IDX_HEADS // 2):
            rel = jnp.maximum(_dot_nt(kid, pairs[p]), 0.0)
            acc = acc + wt[2 * p:2 * p + 1, :] * rel[:, :tq]
            acc = acc + wt[2 * p + 1:2 * p + 2, :] * rel[:, tq:]
        score = jnp.where(krow + kt * tk < lim, acc, NEG_INF)
        key = _sort_key(score)
        keys_ref[kt] = key
        hi_ref[kt] = (key >> 16).astype(jnp.int16)
        lo_ref[kt] = (key ^ 0x8000).astype(jnp.int16)
        return carry

    lax.fori_loop(0, n_kt, idx_body, 0)

    n_part = 4
    sub16 = 16

    def count16(ref, pred, cand):
        candb = jnp.broadcast_to(cand.astype(jnp.int16), (sub16, tq))

        def body(kt, cnts):
            xx = ref[kt]
            cnts = list(cnts)
            for r in range(tk // sub16):
                hit = pred(xx[r * sub16:(r + 1) * sub16, :], candb).astype(jnp.int16)
                cnts[r % n_part] = cnts[r % n_part] + hit
            return tuple(cnts)

        zero = jnp.zeros((sub16, tq), jnp.int16)
        cnts = lax.fori_loop(0, n_kt, body, (zero,) * n_part)
        total = sum(c.astype(F32) for c in cnts)
        return jnp.sum(total, axis=0, keepdims=True)

    ge = lambda a, b: a >= b
    gt = lambda a, b: a > b
    i16_min = -(2 ** 15)

    def search16(ref, need):
        zeros = jnp.zeros((1, tq), jnp.int32)
        thr0 = jnp.where(count16(ref, ge, zeros) >= need, zeros, jnp.full_like(zeros, i16_min))

        def step(it, thr):
            cand = thr | lax.shift_left(jnp.int32(1), 14 - it)
            return jnp.where(count16(ref, ge, cand) >= need, cand, thr)

        return lax.fori_loop(0, 15, step, thr0)

    need = jnp.full((1, tq), float(topk), F32)
    thr_hi = search16(hi_ref, need)
    need_lo = need - count16(hi_ref, gt, thr_hi)

    thr_hib = jnp.broadcast_to(thr_hi.astype(jnp.int16), (tk, tq))

    def keep_lo(kt, carry):
        lo_ref[kt] = jnp.where(hi_ref[kt] == thr_hib, lo_ref[kt], jnp.int16(i16_min))
        return carry

    lax.fori_loop(0, n_kt, keep_lo, 0)
    thr_lo = search16(lo_ref, need_lo)
    thr = (thr_hi << 16) | ((thr_lo ^ 0x8000) & 0xFFFF)
    thr = jnp.maximum(thr, NEG_INF_KEY + 1)
    thr_b = jnp.broadcast_to(thr, (tk, tq))

    q3 = [_stack_heads(q_ref, g) for g in range(N_KV_HEADS)]

    half = tk // 2
    thr_h = thr_b[:half]

    def qk_stage(kt, sub, buf_ref):
        off = pl.multiple_of(kt * tk + sub * half, half)
        for g in range(N_KV_HEADS):
            hs = slice(g * HEAD_DIM, (g + 1) * HEAD_DIM)
            buf_ref[g] = _dot_nt(k_ref[pl.ds(off, half), hs], q3[g])

    def softmax_stage(kt, sub, buf_ref, carry):
        kk = keys_ref[kt, sub * half:(sub + 1) * half, :]
        bias = jnp.where(kk >= thr_h, 0.0, NEG_INF)
        bias = jnp.concatenate([bias] * GROUP, axis=1)
        out = []
        for g in range(N_KV_HEADS):
            m, acc = carry[g]
            raw = buf_ref[g] + bias
            m_new = jnp.maximum(m, jnp.max(raw, axis=0, keepdims=True))
            a = jnp.exp2((m - m_new) * EXP2_SCALE)
            p = jnp.exp2((raw - m_new) * EXP2_SCALE)
            acc = a * acc + _dot(vt_ref[2 * kt + sub, g], p.astype(BF16))
            out.append((m_new, acc))
        return tuple(out)

    def att_body(kt, carry):
        qk_stage(kt, 1, sb_ref)
        carry = softmax_stage(kt, 0, sa_ref, carry)
        qk_stage(jnp.minimum(kt + 1, n_kt - 1), 0, sa_ref)
        return softmax_stage(kt, 1, sb_ref, carry)

    init = tuple((jnp.full((1, GROUP * tq), NEG_INF, F32),
                  jnp.zeros((HEAD_DIM + V_ONES, GROUP * tq), F32)) for _ in range(N_KV_HEADS))
    qk_stage(0, 0, sa_ref)
    res = lax.fori_loop(0, n_kt, att_body, init)
    for g in range(N_KV_HEADS):
        acc = res[g][1]
        o_t = acc[:HEAD_DIM] / acc[HEAD_DIM:HEAD_DIM + 1]
        for r in range(GROUP):
            h = GROUP * g + r
            o_ref[:, h * HEAD_DIM:(h + 1) * HEAD_DIM] = o_t[:, r * tq:(r + 1) * tq].T.astype(o_ref.dtype)

    _mem_attend(qm_ref, mk_ref, mv_ref, o_ref)


def _dsa(q, qm, qi, tail, k, vt, kid, mk, mv, *, s_valid, pos0, topk):
    n = q.shape[0]
    nb, n_half, _, vrows, half = vt.shape
    n_kt, tk = n_half // 2, 2 * half
    s_pad = k.shape[1]
    nq = n // nb // TQ
    n_mem = mk.shape[0] // nb
    row = lambda b, i: (b * nq + i, 0)
    per_b = lambda b, i: (b, 0, 0)
    return pl.pallas_call(
        functools.partial(_dsa_kernel, tk=tk, s_valid=s_valid, pos0=pos0, topk=topk),
        grid=(nb, nq),
        in_specs=[
            pl.BlockSpec((TQ, Q_W), row),
            pl.BlockSpec((TQ, MQ_W), row),
            pl.BlockSpec((TQ, IDXQ_W), row),
            pl.BlockSpec((TQ, LANES), row),
            _resident((None, s_pad, KV_W), per_b),
            _resident((None, n_half, N_KV_HEADS, vrows, half), lambda b, i: (b, 0, 0, 0, 0)),
            _resident((None, s_pad, LANES), per_b),
            pl.BlockSpec((n_mem, MQ_W), lambda b, i: (b, 0)),
            pl.BlockSpec((n_mem, MQ_W), lambda b, i: (b, 0)),
        ],
        out_specs=pl.BlockSpec((TQ, O_W), row),
        out_shape=jax.ShapeDtypeStruct((n, O_W), BF16),
        scratch_shapes=[pltpu.VMEM((n_kt, tk, TQ), jnp.int32),
                        pltpu.VMEM((n_kt, tk, TQ), jnp.int16),
                        pltpu.VMEM((n_kt, tk, TQ), jnp.int16),
                        pltpu.VMEM((N_KV_HEADS, half, GROUP * TQ), F32),
                        pltpu.VMEM((N_KV_HEADS, half, GROUP * TQ), F32)],
        compiler_params=_params(2),
        name="dsa_attend",
    )(q, qm, qi, tail, k, vt, kid, mk, mv)


def _key_tiles_t(v, tk):
    nb, s, _ = v.shape
    vt = v.reshape(nb, s // tk, tk, N_KV_HEADS, HEAD_DIM).transpose(0, 1, 3, 4, 2)
    ones = jnp.ones((nb, s // tk, N_KV_HEADS, V_ONES, tk), v.dtype)
    return jnp.concatenate([vt, ones], axis=3)


def _band_table_kernel(rb_ref, o_ref, *, qc, band, prefix):
    h = pl.program_id(0)
    shape = o_ref.shape
    r = lax.broadcasted_iota(jnp.int32, shape, 0)
    w = lax.broadcasted_iota(jnp.int32, shape, 1)
    idx = jnp.clip(r - w + prefix, -REL_CLIP, REL_CLIP) + REL_CLIP
    lo = (r // qc) * qc
    inband = jnp.logical_and(w >= lo, w < lo + band)

    def body(t, acc):
        return jnp.where(idx == t, rb_ref[h, t], acc)

    acc = lax.fori_loop(0, 2 * REL_CLIP + 1, body, jnp.zeros(shape, F32))
    o_ref[...] = jnp.where(inband, acc, NEG_INF)


def _band_table(rel_bias, rows, width, qc, band, prefix):
    nh = rel_bias.shape[0]
    return pl.pallas_call(
        functools.partial(_band_table_kernel, qc=qc, band=band, prefix=prefix),
        grid=(nh,),
        in_specs=[pl.BlockSpec(memory_space=pltpu.SMEM)],
        out_specs=pl.BlockSpec((None, rows, width), lambda h: (h, 0, 0)),
        out_shape=jax.ShapeDtypeStruct((nh, rows, width), F32),
        compiler_params=_params(1),
        name="band_table",
    )(rel_bias)


def _band_kernel(q_ref, qm_ref, k_ref, v_ref, tab_ref, mk_ref, mv_ref, o_ref, *, tq, width, kp0):
    i = pl.program_id(1)
    off = pl.multiple_of(i * tq, tq)
    kp_ok = (kp0 + i * tq + lax.broadcasted_iota(jnp.int32, (GROUP * tq, width), 1)) >= 0
    for g in range(N_KV_HEADS):
        hs = slice(g * HEAD_DIM, (g + 1) * HEAD_DIM)
        q3 = _stack_heads(q_ref, g)
        s = _dot_nt(q3, k_ref[pl.ds(off, width), hs]) * ATTN_SCALE
        s = s + jnp.concatenate([tab_ref[GROUP * g + r] for r in range(GROUP)], axis=0)
        s = jnp.where(kp_ok, s, NEG_INF)
        m = jnp.max(s, axis=1, keepdims=True)
        p = jnp.exp(s - m)
        l = jnp.sum(p, axis=1, keepdims=True)
        o = _dot(p.astype(BF16), v_ref[pl.ds(off, width), hs]) / l
        _unstack_heads(o_ref, g, o, tq)
    _mem_attend(qm_ref, mk_ref, mv_ref, o_ref)


def _band(q, qm, k_all, v_all, table, mk, mv, *, tq, kp0):
    n = q.shape[0]
    nb, l_pad, _ = k_all.shape
    nq = n // nb // tq
    n_mem = mk.shape[0] // nb
    width = table.shape[2]
    row = lambda b, i: (b * nq + i, 0)
    per_b = lambda b, i: (b, 0, 0)
    return pl.pallas_call(
        functools.partial(_band_kernel, tq=tq, width=width, kp0=kp0),
        grid=(nb, nq),
        in_specs=[
            pl.BlockSpec((tq, Q_W), row),
            pl.BlockSpec((tq, MQ_W), row),
            _resident((None, l_pad, KV_W), per_b),
            _resident((None, l_pad, KV_W), per_b),
            _resident(table.shape, lambda b, i: (0, 0, 0)),
            pl.BlockSpec((n_mem, MQ_W), lambda b, i: (b, 0)),
            pl.BlockSpec((n_mem, MQ_W), lambda b, i: (b, 0)),
        ],
        out_specs=pl.BlockSpec((tq, O_W), row),
        out_shape=jax.ShapeDtypeStruct((n, O_W), BF16),
        compiler_params=_params(2),
        name="band_attend",
    )(q, qm, k_all, v_all, table, mk, mv)


def _pad_rows(a, rows):
    return jnp.pad(a, ((0, 0), (0, rows - a.shape[1]), (0, 0)))


def kernel(x_prompt, x_sample, mem_prompt, cache_a_k, cache_a_v, cache_a_kidx, cache_b_k, cache_b_v,
           cache_mem_k, cache_mem_v, w_in_a, w_in_b, w_o, rel_bias, w_mem_kv,
           w_ffn1_in, w_ffn1_out, w_ffn2_in, w_ffn2_out, ln_g, ln_b):
    depth = w_o.shape[0]
    alpha = (2 * depth) ** 0.25
    bp, tp, d = x_prompt.shape
    bs, ts, _ = x_sample.shape
    past = cache_a_k.shape[2]
    n_mem = mem_prompt.shape[1]
    assert tp % LANES == 0 and ts % 16 == 0 and ts <= CHUNK
    assert cache_b_k.shape[2] == WIN_B and tp >= WIN_B

    pa_pad = -(-PROJ_A // LANES) * LANES
    wa = jnp.pad(w_in_a, ((0, 0), (0, 0), (0, pa_pad - PROJ_A))).astype(BF16)
    wb = w_in_b.astype(BF16)
    wo = w_o.astype(BF16)
    w1i, w1o = w_ffn1_in.astype(BF16), w_ffn1_out.astype(BF16)
    w2i, w2o = w_ffn2_in.astype(BF16), w_ffn2_out.astype(BF16)

    memf, memb = _mem_kv(mem_prompt.reshape(bp * n_mem, d), w_mem_kv.astype(BF16))
    mk_p = memf[..., :MQ_W].reshape(depth, bp, n_mem, N_MEM_HEADS, HEAD_DIM)
    mv_p = memf[..., MQ_W:].reshape(depth, bp, n_mem, N_MEM_HEADS, HEAD_DIM)
    mkb_p, mvb_p = memb[..., :MQ_W], memb[..., MQ_W:]
    mkb_s = cache_mem_k.astype(BF16).reshape(depth, bs * n_mem, MQ_W)
    mvb_s = cache_mem_v.astype(BF16).reshape(depth, bs * n_mem, MQ_W)

    tab_p = _rope_tables(jnp.arange(tp))
    tab_s = jnp.tile(_rope_tables(past + jnp.arange(ts)), (1, bs, 1))

    xp = x_prompt.reshape(bp * tp, d)
    xs = x_sample.reshape(bs * ts, d)
    st = {k: [] for k in ("akp", "avp", "aip", "bkp", "bvp", "aks", "avs", "ais", "bks", "bvs")}
    heads = lambda a, b, t: a.reshape(b, t, N_KV_HEADS, HEAD_DIM)

    for i in range(depth):
        j = i // 2
        g, b = ln_g[i], ln_b[i]
        xp = _ffn_ln(xp, w1i[i], w1o[i], g[0], b[0], alpha)
        xs = _ffn_ln(xs, w1i[i], w1o[i], g[0], b[0], alpha)
        if i % 2 == 0:
            q, kf, vf, kb, vb, qm, qi, tail, kid = _proj(xp, wa[j], tab_p, True)
            tk = min(DSA_TK, tp)
            ap = _dsa(q, qm, qi, tail, kb.reshape(bp, tp, KV_W),
                      _key_tiles_t(vb.reshape(bp, tp, KV_W), tk // 2),
                      kid.reshape(bp, tp, LANES), mkb_p[i], mvb_p[i],
                      s_valid=tp, pos0=0, topk=min(TOPK_MAX, tp // 4))
            st["akp"].append(heads(kf, bp, tp))
            st["avp"].append(heads(vf, bp, tp))
            st["aip"].append(tail[:, :IDX_DIM].reshape(bp, tp, IDX_DIM))

            q, kf, vf, kb, vb, qm, qi, tail, kid = _proj(xs, wa[j], tab_s, True)
            tk = DSA_TK
            s_valid = past + ts
            s_pad = -(-s_valid // tk) * tk
            ci = cache_a_kidx[j].astype(BF16)
            k_all = _pad_rows(jnp.concatenate(
                [cache_a_k[j].astype(BF16).reshape(bs, past, KV_W), kb.reshape(bs, ts, KV_W)], 1), s_pad)
            v_all = _pad_rows(jnp.concatenate(
                [cache_a_v[j].astype(BF16).reshape(bs, past, KV_W), vb.reshape(bs, ts, KV_W)], 1), s_pad)
            kid_all = _pad_rows(jnp.concatenate(
                [jnp.concatenate([ci, ci], -1), kid.reshape(bs, ts, LANES)], 1), s_pad)
            qpad = lambda a: _pad_rows(a.reshape(bs, ts, a.shape[-1]), TQ).reshape(bs * TQ, a.shape[-1])
            as_ = _dsa(qpad(q), qpad(qm), qpad(qi), qpad(tail), k_all, _key_tiles_t(v_all, tk // 2), kid_all,
                       mkb_s[i], mvb_s[i], s_valid=s_valid, pos0=past, topk=min(TOPK_MAX, s_valid // 4))
            as_ = as_.reshape(bs, TQ, O_W)[:, :ts].reshape(bs * ts, O_W)
            st["aks"].append(heads(kf, bs, ts))
            st["avs"].append(heads(vf, bs, ts))
            st["ais"].append(tail[:, :IDX_DIM].reshape(bs, ts, IDX_DIM))
        else:
            q, kf, vf, kb, vb, qm = _proj(xp, wb[j], None, False)
            tq = LANES
            table = _band_table(rel_bias[j], tq, tq + WIN_B, CHUNK, WIN_B + CHUNK, WIN_B)
            zpad = jnp.zeros((bp, WIN_B, KV_W), BF16)
            ap = _band(q, qm, jnp.concatenate([zpad, kb.reshape(bp, tp, KV_W)], 1),
                       jnp.concatenate([zpad, vb.reshape(bp, tp, KV_W)], 1),
                       table, mkb_p[i], mvb_p[i], tq=tq, kp0=-WIN_B)
            n_keep = min(WIN_B, tp)
            st["bkp"].append(heads(kf, bp, tp)[:, tp - n_keep:])
            st["bvp"].append(heads(vf, bp, tp)[:, tp - n_keep:])

            q, kf, vf, kb, vb, qm = _proj(xs, wb[j], None, False)
            width = -(-(WIN_B + ts) // LANES) * LANES
            table = _band_table(rel_bias[j], ts, width, ts, WIN_B + ts, WIN_B)
            k_all = _pad_rows(jnp.concatenate(
                [cache_b_k[j].astype(BF16).reshape(bs, WIN_B, KV_W), kb.reshape(bs, ts, KV_W)], 1), width)
            v_all = _pad_rows(jnp.concatenate(
                [cache_b_v[j].astype(BF16).reshape(bs, WIN_B, KV_W), vb.reshape(bs, ts, KV_W)], 1), width)
            as_ = _band(q, qm, k_all, v_all, table, mkb_s[i], mvb_s[i], tq=ts, kp0=past - WIN_B)
            st["bks"].append(jnp.concatenate([cache_b_k[j][:, ts:], heads(kf, bs, ts)], 1))
            st["bvs"].append(jnp.concatenate([cache_b_v[j][:, ts:], heads(vf, bs, ts)], 1))

        xp2 = _oproj_ln(ap, wo[i], xp, g[1], b[1], alpha)
        xs2 = _oproj_ln(as_, wo[i], xs, g[1], b[1], alpha)
        xp = _ffn_ln(xp2, w2i[i], w2o[i], g[2], b[2], alpha)
        xs = _ffn_ln(xs2, w2i[i], w2o[i], g[2], b[2], alpha)

    stack = lambda name: jnp.stack(st[name])
    return (xp.reshape(bp, tp, d), xs.reshape(bs, ts, d),
            stack("akp"), stack("avp"), stack("aip"), stack("bkp"), stack("bvp"), mk_p, mv_p,
            stack("aks"), stack("avs"), stack("ais"), stack("bks"), stack("bvs"))
```

```python
import functools

import numpy as np
import jax
import jax.numpy as jnp
from jax import lax
from jax.experimental import pallas as pl
from jax.experimental.pallas import tpu as pltpu

CHUNK = 64
CHUNK_SHIFT = 6
N_HEADS = 12
N_KV_HEADS = 4
HEAD_DIM = 128
GROUP = N_HEADS // N_KV_HEADS
N_MEM_HEADS = 4
IDX_HEADS = 8
IDX_DIM = 64
TOPK_MAX = 256
WIN_B = 8 * CHUNK
REL_CLIP = 128
ROPE_THETA = 10000.0
LN_EPS = 1e-5
Q_W = N_HEADS * HEAD_DIM
KV_W = N_KV_HEADS * HEAD_DIM
MQ_W = N_MEM_HEADS * HEAD_DIM
IDXQ_W = IDX_HEADS * IDX_DIM
O_W = Q_W + MQ_W
PROJ_A = Q_W + 2 * KV_W + MQ_W + IDXQ_W + IDX_DIM + IDX_HEADS
PROJ_B = Q_W + 2 * KV_W + MQ_W
ATTN_SCALE = HEAD_DIM ** -0.5
IDX_W_SCALE = (IDX_HEADS * IDX_DIM) ** -0.5
NEG_INF = -1e30
EXP2_SCALE = ATTN_SCALE * float(np.log2(np.e))

LANES = 128
FFN_TM = 1024
FFN_TF = 256
TQ = 2 * LANES
DSA_TK = 4 * LANES
DSA_HALF = DSA_TK // 2
V_ONES = 16
VMEM_LIMIT = 62 * 1024 * 1024

F32 = jnp.float32
BF16 = jnp.bfloat16
INT_MIN = -(2 ** 31)
NEG_INF_KEY = int(np.array(NEG_INF, np.float32).view(np.int32)) ^ 0x7FFFFFFF


def _dot(a, b):
    return jnp.dot(a, b, preferred_element_type=F32)


def _dot_nt(a, b):
    return lax.dot_general(a, b, (((1,), (1,)), ((), ())), preferred_element_type=F32)


def _params(n_axes):
    return pltpu.CompilerParams(dimension_semantics=("arbitrary",) * n_axes,
                                vmem_limit_bytes=VMEM_LIMIT)


def _resident(block_shape, index_map):
    return pl.BlockSpec(block_shape, index_map, pipeline_mode=pl.Buffered(1))


def _layer_norm(y, g, b):
    mu = jnp.mean(y, axis=-1, keepdims=True)
    d = y - mu
    var = jnp.mean(d * d, axis=-1, keepdims=True)
    return d * lax.rsqrt(var + LN_EPS) * g + b


def _ffn_ln_kernel(x_ref, wg_ref, wu_ref, wo_ref, g_ref, b_ref, o_ref, xb_ref, *, alpha):
    j = pl.program_id(1)

    @pl.when(j == 0)
    def _():
        xb_ref[...] = x_ref[...].astype(BF16)
        o_ref[...] = jnp.zeros_like(o_ref)

    xb = xb_ref[...]
    gate = _dot(xb, wg_ref[...])
    up = _dot(xb, wu_ref[...])
    h = (gate / (1.0 + jnp.exp(-gate))) * up
    o_ref[...] += _dot(h.astype(BF16), wo_ref[...])

    @pl.when(j == pl.num_programs(1) - 1)
    def _():
        y = alpha * x_ref[...] + 0.5 * o_ref[...]
        o_ref[...] = _layer_norm(y, g_ref[...], b_ref[...])


def _ffn_ln(x, w_in, w_out, g, b, alpha):
    n, d = x.shape
    dff = w_out.shape[0]
    tm = min(FFN_TM, n)
    tf = FFN_TF if dff % FFN_TF == 0 else LANES
    nj = dff // tf
    return pl.pallas_call(
        functools.partial(_ffn_ln_kernel, alpha=alpha),
        grid=(n // tm, nj),
        in_specs=[
            pl.BlockSpec((tm, d), lambda i, j: (i, 0)),
            pl.BlockSpec((d, tf), lambda i, j: (0, j)),
            pl.BlockSpec((d, tf), lambda i, j: (0, j + nj)),
            pl.BlockSpec((tf, d), lambda i, j: (j, 0)),
            pl.BlockSpec((1, d), lambda i, j: (0, 0)),
            pl.BlockSpec((1, d), lambda i, j: (0, 0)),
        ],
        out_specs=pl.BlockSpec((tm, d), lambda i, j: (i, 0)),
        out_shape=jax.ShapeDtypeStruct((n, d), F32),
        scratch_shapes=[pltpu.VMEM((tm, d), BF16)],
        compiler_params=_params(2),
        name="ffn_ln",
    )(x, w_in, w_in, w_out, g.reshape(1, d), b.reshape(1, d))


def _oproj_ln_kernel(a_ref, w_ref, x_ref, g_ref, b_ref, o_ref, *, alpha):
    y = alpha * x_ref[...] + _dot(a_ref[...], w_ref[...])
    o_ref[...] = _layer_norm(y, g_ref[...], b_ref[...])


def _oproj_ln(a, w, x, g, b, alpha):
    n, d = x.shape
    ow = a.shape[1]
    tm = min(512, n)
    return pl.pallas_call(
        functools.partial(_oproj_ln_kernel, alpha=alpha),
        grid=(n // tm,),
        in_specs=[
            pl.BlockSpec((tm, ow), lambda i: (i, 0)),
            _resident((ow, d), lambda i: (0, 0)),
            pl.BlockSpec((tm, d), lambda i: (i, 0)),
            pl.BlockSpec((1, d), lambda i: (0, 0)),
            pl.BlockSpec((1, d), lambda i: (0, 0)),
        ],
        out_specs=pl.BlockSpec((tm, d), lambda i: (i, 0)),
        out_shape=jax.ShapeDtypeStruct((n, d), F32),
        compiler_params=_params(1),
        name="oproj_ln",
    )(a, w, x, g.reshape(1, d), b.reshape(1, d))


def _mem_kv_kernel(x_ref, w_ref, of_ref, ob_ref):
    y = _dot(x_ref[...].astype(BF16), w_ref[...])
    of_ref[...] = y
    ob_ref[...] = y.astype(BF16)


def _mem_kv(mem, w):
    n, d = mem.shape
    depth, _, c = w.shape
    return pl.pallas_call(
        _mem_kv_kernel,
        grid=(depth,),
        in_specs=[
            _resident((n, d), lambda l: (0, 0)),
            pl.BlockSpec((None, d, c), lambda l: (l, 0, 0)),
        ],
        out_specs=[
            pl.BlockSpec((None, n, c), lambda l: (l, 0, 0)),
            pl.BlockSpec((None, n, c), lambda l: (l, 0, 0)),
        ],
        out_shape=[jax.ShapeDtypeStruct((depth, n, c), F32),
                   jax.ShapeDtypeStruct((depth, n, c), BF16)],
        compiler_params=_params(1),
        name="mem_kv",
    )(mem, w)


def _rope_tables(pos):
    posf = pos.astype(F32)[:, None]

    def cs(half):
        inv_freq = ROPE_THETA ** (-jnp.arange(half, dtype=F32) / half)
        ang = posf * inv_freq[None, :]
        return jnp.cos(ang), jnp.sin(ang)

    c64, s64 = cs(HEAD_DIM // 2)
    c32, s32 = cs(IDX_DIM // 2)
    z32 = jnp.zeros_like(s32)
    cos128 = jnp.concatenate([c64, c64], -1)
    sin128 = jnp.concatenate([-s64, s64], -1)
    cos64 = jnp.concatenate([c32] * 4, -1)
    sa64 = jnp.concatenate([-s32, z32, -s32, z32], -1)
    sb64 = jnp.concatenate([z32, s32, z32, s32], -1)
    lane = jnp.arange(LANES)[None, :]
    first = lane < IDX_DIM
    return jnp.stack([cos128, sin128, cos64, sa64, sb64,
                      jnp.where(first, cos64, 1.0), jnp.where(first, sa64, 0.0),
                      jnp.where(first, sb64, 0.0)])


def _proj_kernel(x_ref, w_ref, *refs, layer_a, emit_vt):
    xb = x_ref[...].astype(BF16)
    vt_ref = None
    if layer_a:
        tab_ref, q_ref, kf_ref, vf_ref, kb_ref, vb_ref, qm_ref, qi_ref, tail_ref, kid_ref = refs[:10]
        if emit_vt:
            vt_ref = refs[10]
    else:
        q_ref, kf_ref, vf_ref, kb_ref, vb_ref, qm_ref = refs

    def rope128(y):
        return y * tab_ref[0] + pltpu.roll(y, HEAD_DIM // 2, axis=1) * tab_ref[1]

    def rope64(y, t0):
        return (y * tab_ref[t0] + pltpu.roll(y, LANES - IDX_DIM // 2, axis=1) * tab_ref[t0 + 1]
                + pltpu.roll(y, IDX_DIM // 2, axis=1) * tab_ref[t0 + 2])

    def section(col0, width, fn):
        step = 512
        for c0 in range(0, width, step):
            cw = min(step, width - c0)
            y = _dot(xb, w_ref[:, col0 + c0:col0 + c0 + cw])
            for s in range(cw // LANES):
                fn(c0 + s * LANES, y[:, s * LANES:(s + 1) * LANES])

    def put(ref, fn=None):
        def f(off, y):
            ref[:, off:off + LANES] = (y if fn is None else fn(y)).astype(ref.dtype)
        return f

    def put2(ref_f, ref_b, fn=None):
        def f(off, y):
            y = y if fn is None else fn(y)
            ref_f[:, off // HEAD_DIM, :] = y
            ref_b[:, off:off + LANES] = y.astype(BF16)
        return f

    put_v = put2(vf_ref, vb_ref)

    def put_v_and_vt(off, y):
        put_v(off, y)
        g = off // HEAD_DIM
        for u in range(y.shape[0] // DSA_HALF):
            vt_ref[u, g, :HEAD_DIM, :] = y[u * DSA_HALF:(u + 1) * DSA_HALF, :].T.astype(BF16)
            vt_ref[u, g, HEAD_DIM:, :] = jnp.ones((V_ONES, DSA_HALF), BF16)

    rq = rope128 if layer_a else None
    section(0, Q_W, put(q_ref, rq))
    section(Q_W, KV_W, put2(kf_ref, kb_ref, rq))
    section(Q_W + KV_W, KV_W, put_v_and_vt if emit_vt else put_v)
    section(Q_W + 2 * KV_W, MQ_W, put(qm_ref))
    if layer_a:
        c = Q_W + 2 * KV_W + MQ_W
        section(c, IDXQ_W, put(qi_ref, lambda y: rope64(y, 2)))

        def tail(off, y):
            t = rope64(y, 5)
            tail_ref[...] = t
            lane = lax.broadcasted_iota(jnp.int32, t.shape, 1)
            dup = jnp.where(lane < IDX_DIM, t, pltpu.roll(t, IDX_DIM, axis=1))
            kid_ref[...] = dup.astype(BF16)
        section(c + IDXQ_W, LANES, tail)


def _proj(x, w, tab, layer_a):
    n, d = x.shape
    pw = w.shape[1]
    tm = min(512, n)
    row = lambda i: (i, 0)
    in_specs = [pl.BlockSpec((tm, d), row), _resident((d, pw), lambda i: (0, 0))]
    args = [x, w]
    kv_state = (N_KV_HEADS, HEAD_DIM)
    outs = [((Q_W,), BF16), (kv_state, F32), (kv_state, F32), ((KV_W,), BF16), ((KV_W,), BF16),
            ((MQ_W,), BF16)]
    if layer_a:
        nt = tab.shape[1] // tm
        in_specs.append(pl.BlockSpec((8, tm, LANES), lambda i: (0, i % nt, 0)))
        args.append(tab)
        outs += [((IDXQ_W,), BF16), ((LANES,), F32), ((LANES,), BF16)]
    emit_vt = layer_a and tm % DSA_HALF == 0
    out_specs = [pl.BlockSpec((tm,) + wd, lambda i, nd=len(wd): (i,) + (0,) * nd) for wd, _ in outs]
    out_shape = [jax.ShapeDtypeStruct((n,) + wd, dt) for wd, dt in outs]
    if emit_vt:
        vt_tile = (N_KV_HEADS, HEAD_DIM + V_ONES, DSA_HALF)
        out_specs.append(pl.BlockSpec((tm // DSA_HALF,) + vt_tile, lambda i: (i, 0, 0, 0)))
        out_shape.append(jax.ShapeDtypeStruct((n // DSA_HALF,) + vt_tile, BF16))
    return pl.pallas_call(
        functools.partial(_proj_kernel, layer_a=layer_a, emit_vt=emit_vt),
        grid=(n // tm,),
        in_specs=in_specs,
        out_specs=out_specs,
        out_shape=out_shape,
        compiler_params=_params(1),
        name="proj_a" if layer_a else "proj_b",
    )(*args)


def _stack_heads(q_ref, g):
    return jnp.concatenate(
        [q_ref[:, (GROUP * g + r) * HEAD_DIM:(GROUP * g + r + 1) * HEAD_DIM] for r in range(GROUP)],
        axis=0)


def _unstack_heads(o_ref, g, o, tq):
    for r in range(GROUP):
        h = GROUP * g + r
        o_ref[:, h * HEAD_DIM:(h + 1) * HEAD_DIM] = o[r * tq:(r + 1) * tq].astype(o_ref.dtype)


def _mem_attend(qm_ref, mk_ref, mv_ref, o_ref):
    for hm in range(N_MEM_HEADS):
        sl = slice(hm * HEAD_DIM, (hm + 1) * HEAD_DIM)
        s = _dot_nt(qm_ref[:, sl], mk_ref[:, sl]) * ATTN_SCALE
        m = jnp.max(s, axis=1, keepdims=True)
        p = jnp.exp(s - m)
        l = jnp.sum(p, axis=1, keepdims=True)
        om = _dot(p.astype(BF16), mv_ref[:, sl]) / l
        o_ref[:, Q_W + hm * HEAD_DIM:Q_W + (hm + 1) * HEAD_DIM] = om.astype(o_ref.dtype)


def _sort_key(x):
    bits = lax.bitcast_convert_type(x, jnp.int32)
    return bits ^ ((bits >> 31) & 0x7FFFFFFF)


def _dsa_kernel(q_ref, qm_ref, qi_ref, tail_ref, k_ref, vt_ref, kid_ref, mk_ref, mv_ref,
                o_ref, keys_ref, hi_ref, lo_ref, sa_ref, sb_ref, *, tk, s_valid, pos0, topk):
    tq = TQ
    i = pl.program_id(1)
    row0 = pos0 + i * tq
    qpos = row0 + lax.broadcasted_iota(jnp.int32, (1, tq), 1)
    lim = jnp.minimum(((qpos >> CHUNK_SHIFT) + 1) * CHUNK, s_valid)
    lim_max = jnp.minimum((((row0 + tq - 1) >> CHUNK_SHIFT) + 1) * CHUNK, s_valid)
    n_kt = (lim_max + tk - 1) // tk

    lane = lax.broadcasted_iota(jnp.int32, (tq, LANES), 1)
    pairs = []
    for p in range(IDX_HEADS // 2):
        qp = qi_ref[:, p * LANES:(p + 1) * LANES]
        zero = jnp.zeros_like(qp)
        pairs.append(jnp.concatenate(
            [jnp.where(lane < IDX_DIM, qp, zero), jnp.where(lane >= IDX_DIM, qp, zero)], axis=0))
    wt = tail_ref[...].T[IDX_DIM:IDX_DIM + IDX_HEADS, :] * IDX_W_SCALE
    krow = lax.broadcasted_iota(jnp.int32, (tk, tq), 0)

    def idx_body(kt, carry):
        kid = kid_ref[pl.ds(pl.multiple_of(kt * tk, tk), tk), :]
        acc = jnp.zeros((tk, tq), F32)
        for p in range(IDX_HEADS // 2):
            rel = jnp.maximum(_dot_nt(kid, pairs[p]), 0.0)
            acc = acc + wt[2 * p:2 * p + 1, :] * rel[:, :tq]
            acc = acc + wt[2 * p + 1:2 * p + 2, :] * rel[:, tq:]
        score = jnp.where(krow + kt * tk < lim, acc, NEG_INF)
        key = _sort_key(score)
        keys_ref[kt] = key
        hi_ref[kt] = (key >> 16).astype(jnp.int16)
        lo_ref[kt] = (key ^ 0x8000).astype(jnp.int16)
        return carry

    lax.fori_loop(0, n_kt, idx_body, 0)

    n_part = 4
    sub16 = 16

    def count16(ref, pred, cand):
        candb = jnp.broadcast_to(cand.astype(jnp.int16), (sub16, tq))

        def body(kt, cnts):
            xx = ref[kt]
            cnts = list(cnts)
            for r in range(tk // sub16):
                hit = pred(xx[r * sub16:(r + 1) * sub16, :], candb).astype(jnp.int16)
                cnts[r % n_part] = cnts[r % n_part] + hit
            return tuple(cnts)

        zero = jnp.zeros((sub16, tq), jnp.int16)
        cnts = lax.fori_loop(0, n_kt, body, (zero,) * n_part)
        total = sum(c.astype(F32) for c in cnts)
        return jnp.sum(total, axis=0, keepdims=True)

    ge = lambda a, b: a >= b
    gt = lambda a, b: a > b
    i16_min = -(2 ** 15)

    def search16(ref, need):
        zeros = jnp.zeros((1, tq), jnp.int32)
        thr0 = jnp.where(count16(ref, ge, zeros) >= need, zeros, jnp.full_like(zeros, i16_min))

        def step(it, thr):
            cand = thr | lax.shift_left(jnp.int32(1), 14 - it)
            return jnp.where(count16(ref, ge, cand) >= need, cand, thr)

        return lax.fori_loop(0, 15, step, thr0)

    need = jnp.full((1, tq), float(topk), F32)
    thr_hi = search16(hi_ref, need)
    need_lo = need - count16(hi_ref, gt, thr_hi)

    thr_hib = jnp.broadcast_to(thr_hi.astype(jnp.int16), (tk, tq))

    def keep_lo(kt, carry):
        lo_ref[kt] = jnp.where(hi_ref[kt] == thr_hib, lo_ref[kt], jnp.int16(i16_min))
        return carry

    lax.fori_loop(0, n_kt, keep_lo, 0)
    thr_lo = search16(lo_ref, need_lo)
    thr = (thr_hi << 16) | ((thr_lo ^ 0x8000) & 0xFFFF)
    thr_b = jnp.broadcast_to(thr, (tk, tq))

    def count32(pred):
        def body(kt, cnt):
            hit = jnp.where(pred(keys_ref[kt], kt * tk), 1, 0)
            for r in range(tk // 8):
                cnt = cnt + hit[r * 8:(r + 1) * 8, :]
            return cnt
        cnt = lax.fori_loop(0, n_kt, body, jnp.zeros((8, tq), jnp.int32))
        return jnp.sum(cnt.astype(F32), axis=0, keepdims=True)

    surplus = jnp.where(thr > NEG_INF_KEY, count32(lambda kk, k0: kk >= thr_b) - need, 0.0)

    @pl.when(jnp.max(surplus) > 0.0)
    def _():
        keep = count32(lambda kk, k0: kk == thr_b) - surplus
        idx_bits = (keys_ref.shape[0] * tk).bit_length()

        def step(it, bound):
            cand = bound | lax.shift_left(jnp.int32(1), idx_bits - 1 - it)
            cand_b = jnp.broadcast_to(cand, (tk, tq))
            below = count32(lambda kk, k0: jnp.logical_and(kk == thr_b, krow + k0 < cand_b))
            return jnp.where(below <= keep, cand, bound)

        bound = lax.fori_loop(0, idx_bits, step, jnp.zeros((1, tq), jnp.int32))
        bound_b = jnp.broadcast_to(bound, (tk, tq))

        def demote(kt, carry):
            kk = keys_ref[kt]
            drop = jnp.logical_and(kk == thr_b, krow + kt * tk >= bound_b)
            keys_ref[kt] = jnp.where(drop, thr_b - 1, kk)
            return carry

        lax.fori_loop(0, n_kt, demote, 0)

    thr = jnp.maximum(thr, NEG_INF_KEY + 1)

    q3 = [_stack_heads(q_ref, g) for g in range(N_KV_HEADS)]

    half = tk // 2
    thr_h = jnp.broadcast_to(thr, (half, tq))

    def qk_stage(kt, sub, buf_ref):
        off = pl.multiple_of(kt * tk + sub * half, half)
        kk = keys_ref[kt, sub * half:(sub + 1) * half, :]
        bias = jnp.where(kk >= thr_h, 0.0, NEG_INF)
        bias = jnp.concatenate([bias] * GROUP, axis=1)
        for g in range(N_KV_HEADS):
            hs = slice(g * HEAD_DIM, (g + 1) * HEAD_DIM)
            buf_ref[g] = _dot_nt(k_ref[pl.ds(off, half), hs], q3[g]) + bias

    def softmax_stage(kt, sub, buf_ref, carry):
        out = []
        for g in range(N_KV_HEADS):
            m, acc = carry[g]
            m_new = jnp.maximum(m, jnp.max(buf_ref[g], axis=0, keepdims=True))
            a = jnp.exp2((m - m_new) * EXP2_SCALE)
            p = jnp.exp2((buf_ref[g] - m_new) * EXP2_SCALE)
            acc = a * acc + _dot(vt_ref[2 * kt + sub, g], p.astype(BF16))
            out.append((m_new, acc))
        return tuple(out)

    def att_body(kt, carry):
        qk_stage(kt, 1, sb_ref)
        carry = softmax_stage(kt, 0, sa_ref, carry)
        qk_stage(jnp.minimum(kt + 1, n_kt - 1), 0, sa_ref)
        return softmax_stage(kt, 1, sb_ref, carry)

    init = tuple((jnp.full((1, GROUP * tq), NEG_INF, F32),
                  jnp.zeros((HEAD_DIM + V_ONES, GROUP * tq), F32)) for _ in range(N_KV_HEADS))
    qk_stage(0, 0, sa_ref)
    res = lax.fori_loop(0, n_kt, att_body, init)
    for g in range(N_KV_HEADS):
        acc = res[g][1]
        o_t = acc[:HEAD_DIM] / acc[HEAD_DIM:HEAD_DIM + 1]
        for r in range(GROUP):
            h = GROUP * g + r
            o_ref[:, h * HEAD_DIM:(h + 1) * HEAD_DIM] = o_t[:, r * tq:(r + 1) * tq].T.astype(o_ref.dtype)

    _mem_attend(qm_ref, mk_ref, mv_ref, o_ref)


def _dsa(q, qm, qi, tail, k, vt, kid, mk, mv, *, s_valid, pos0, topk):
    n = q.shape[0]
    nb, n_half, _, vrows, half = vt.shape
    n_kt, tk = n_half // 2, 2 * half
    s_pad = k.shape[1]
    nq = n // nb // TQ
    n_mem = mk.shape[0] // nb
    row = lambda b, i: (b * nq + i, 0)
    per_b = lambda b, i: (b, 0, 0)
    return pl.pallas_call(
        functools.partial(_dsa_kernel, tk=tk, s_valid=s_valid, pos0=pos0, topk=topk),
        grid=(nb, nq),
        in_specs=[
            pl.BlockSpec((TQ, Q_W), row),
            pl.BlockSpec((TQ, MQ_W), row),
            pl.BlockSpec((TQ, IDXQ_W), row),
            pl.BlockSpec((TQ, LANES), row),
            _resident((None, s_pad, KV_W), per_b),
            _resident((None, n_half, N_KV_HEADS, vrows, half), lambda b, i: (b, 0, 0, 0, 0)),
            _resident((None, s_pad, LANES), per_b),
            pl.BlockSpec((n_mem, MQ_W), lambda b, i: (b, 0)),
            pl.BlockSpec((n_mem, MQ_W), lambda b, i: (b, 0)),
        ],
        out_specs=pl.BlockSpec((TQ, O_W), row),
        out_shape=jax.ShapeDtypeStruct((n, O_W), BF16),
        scratch_shapes=[pltpu.VMEM((n_kt, tk, TQ), jnp.int32),
                        pltpu.VMEM((n_kt, tk, TQ), jnp.int16),
                        pltpu.VMEM((n_kt, tk, TQ), jnp.int16),
                        pltpu.VMEM((N_KV_HEADS, half, GROUP * TQ), F32),
                        pltpu.VMEM((N_KV_HEADS, half, GROUP * TQ), F32)],
        compiler_params=_params(2),
        name="dsa_attend",
    )(q, qm, qi, tail, k, vt, kid, mk, mv)


def _key_tiles_t(v, tk):
    nb, s, _ = v.shape
    vt = v.reshape(nb, s // tk, tk, N_KV_HEADS, HEAD_DIM).transpose(0, 1, 3, 4, 2)
    ones = jnp.ones((nb, s // tk, N_KV_HEADS, V_ONES, tk), v.dtype)
    return jnp.concatenate([vt, ones], axis=3)


def _band_table_kernel(rb_ref, o_ref, *, qc, band, prefix):
    h = pl.program_id(0)
    shape = o_ref.shape
    r = lax.broadcasted_iota(jnp.int32, shape, 0)
    w = lax.broadcasted_iota(jnp.int32, shape, 1)
    idx = jnp.clip(r - w + prefix, -REL_CLIP, REL_CLIP) + REL_CLIP
    lo = (r // qc) * qc
    inband = jnp.logical_and(w >= lo, w < lo + band)

    def body(t, acc):
        return jnp.where(idx == t, rb_ref[h, t], acc)

    acc = lax.fori_loop(0, 2 * REL_CLIP + 1, body, jnp.zeros(shape, F32))
    o_ref[...] = jnp.where(inband, acc, NEG_INF)


def _band_table(rel_bias, rows, width, qc, band, prefix):
    nh = rel_bias.shape[0]
    return pl.pallas_call(
        functools.partial(_band_table_kernel, qc=qc, band=band, prefix=prefix),
        grid=(nh,),
        in_specs=[pl.BlockSpec(memory_space=pltpu.SMEM)],
        out_specs=pl.BlockSpec((None, rows, width), lambda h: (h, 0, 0)),
        out_shape=jax.ShapeDtypeStruct((nh, rows, width), F32),
        compiler_params=_params(1),
        name="band_table",
    )(rel_bias)


def _band_kernel(q_ref, qm_ref, k_ref, v_ref, tab_ref, mk_ref, mv_ref, o_ref, *, tq, width, kp0):
    i = pl.program_id(1)
    off = pl.multiple_of(i * tq, tq)
    kp_ok = (kp0 + i * tq + lax.broadcasted_iota(jnp.int32, (GROUP * tq, width), 1)) >= 0
    for g in range(N_KV_HEADS):
        hs = slice(g * HEAD_DIM, (g + 1) * HEAD_DIM)
        q3 = _stack_heads(q_ref, g)
        s = _dot_nt(q3, k_ref[pl.ds(off, width), hs]) * ATTN_SCALE
        s = s + jnp.concatenate([tab_ref[GROUP * g + r] for r in range(GROUP)], axis=0)
        s = jnp.where(kp_ok, s, NEG_INF)
        m = jnp.max(s, axis=1, keepdims=True)
        p = jnp.exp(s - m)
        l = jnp.sum(p, axis=1, keepdims=True)
        o = _dot(p.astype(BF16), v_ref[pl.ds(off, width), hs]) / l
        _unstack_heads(o_ref, g, o, tq)
    _mem_attend(qm_ref, mk_ref, mv_ref, o_ref)


def _band(q, qm, k_all, v_all, table, mk, mv, *, tq, kp0):
    n = q.shape[0]
    nb, l_pad, _ = k_all.shape
    nq = n // nb // tq
    n_mem = mk.shape[0] // nb
    width = table.shape[2]
    row = lambda b, i: (b * nq + i, 0)
    per_b = lambda b, i: (b, 0, 0)
    return pl.pallas_call(
        functools.partial(_band_kernel, tq=tq, width=width, kp0=kp0),
        grid=(nb, nq),
        in_specs=[
            pl.BlockSpec((tq, Q_W), row),
            pl.BlockSpec((tq, MQ_W), row),
            _resident((None, l_pad, KV_W), per_b),
            _resident((None, l_pad, KV_W), per_b),
            _resident(table.shape, lambda b, i: (0, 0, 0)),
            pl.BlockSpec((n_mem, MQ_W), lambda b, i: (b, 0)),
            pl.BlockSpec((n_mem, MQ_W), lambda b, i: (b, 0)),
        ],
        out_specs=pl.BlockSpec((tq, O_W), row),
        out_shape=jax.ShapeDtypeStruct((n, O_W), BF16),
        compiler_params=_params(2),
        name="band_attend",
    )(q, qm, k_all, v_all, table, mk, mv)


def _pad_rows(a, rows):
    return jnp.pad(a, ((0, 0), (0, rows - a.shape[1]), (0, 0)))


def kernel(x_prompt, x_sample, mem_prompt, cache_a_k, cache_a_v, cache_a_kidx, cache_b_k, cache_b_v,
           cache_mem_k, cache_mem_v, w_in_a, w_in_b, w_o, rel_bias, w_mem_kv,
           w_ffn1_in, w_ffn1_out, w_ffn2_in, w_ffn2_out, ln_g, ln_b):
    depth = w_o.shape[0]
    alpha = (2 * depth) ** 0.25
    bp, tp, d = x_prompt.shape
    bs, ts, _ = x_sample.shape
    past = cache_a_k.shape[2]
    n_mem = mem_prompt.shape[1]
    assert tp % DSA_TK == 0 and ts % 16 == 0 and ts <= CHUNK
    assert cache_b_k.shape[2] == WIN_B and tp >= WIN_B

    pa_pad = -(-PROJ_A // LANES) * LANES
    layers = lambda w: [w[l].astype(BF16) for l in range(w.shape[0])]
    wa = layers(jnp.pad(w_in_a, ((0, 0), (0, 0), (0, pa_pad - PROJ_A))))
    wb, wo = layers(w_in_b), layers(w_o)
    w1i, w1o = layers(w_ffn1_in), layers(w_ffn1_out)
    w2i, w2o = layers(w_ffn2_in), layers(w_ffn2_out)

    memf, memb = _mem_kv(mem_prompt.reshape(bp * n_mem, d), w_mem_kv.astype(BF16))
    mk_p = memf[..., :MQ_W].reshape(depth, bp, n_mem, N_MEM_HEADS, HEAD_DIM)
    mv_p = memf[..., MQ_W:].reshape(depth, bp, n_mem, N_MEM_HEADS, HEAD_DIM)
    mkb_p, mvb_p = memb[..., :MQ_W], memb[..., MQ_W:]
    mkb_s = cache_mem_k.astype(BF16).reshape(depth, bs * n_mem, MQ_W)
    mvb_s = cache_mem_v.astype(BF16).reshape(depth, bs * n_mem, MQ_W)

    tab_p = _rope_tables(jnp.arange(tp))
    tab_s = jnp.tile(_rope_tables(past + jnp.arange(ts)), (1, bs, 1))

    xp = x_prompt.reshape(bp * tp, d)
    xs = x_sample.reshape(bs * ts, d)
    st = {k: [] for k in ("akp", "avp", "aip", "bkp", "bvp", "aks", "avs", "ais", "bks", "bvs")}
    heads = lambda a, b, t: a.reshape(b, t, N_KV_HEADS, HEAD_DIM)

    for i in range(depth):
        j = i // 2
        g, b = ln_g[i], ln_b[i]
        xp = _ffn_ln(xp, w1i[i], w1o[i], g[0], b[0], alpha)
        xs = _ffn_ln(xs, w1i[i], w1o[i], g[0], b[0], alpha)
        if i % 2 == 0:
            q, kf, vf, kb, vb, qm, qi, tail, kid, vt = _proj(xp, wa[j], tab_p, True)
            ap = _dsa(q, qm, qi, tail, kb.reshape(bp, tp, KV_W),
                      vt.reshape((bp, tp // DSA_HALF) + vt.shape[1:]),
                      kid.reshape(bp, tp, LANES), mkb_p[i], mvb_p[i],
                      s_valid=tp, pos0=0, topk=min(TOPK_MAX, tp // 4))
            st["akp"].append(heads(kf, bp, tp))
            st["avp"].append(heads(vf, bp, tp))
            st["aip"].append(tail[:, :IDX_DIM].reshape(bp, tp, IDX_DIM))

            q, kf, vf, kb, vb, qm, qi, tail, kid = _proj(xs, wa[j], tab_s, True)
            tk = DSA_TK
            s_valid = past + ts
            s_pad = -(-s_valid // tk) * tk
            ci = cache_a_kidx[j].astype(BF16)
            k_all = _pad_rows(jnp.concatenate(
                [cache_a_k[j].astype(BF16).reshape(bs, past, KV_W), kb.reshape(bs, ts, KV_W)], 1), s_pad)
            v_all = _pad_rows(jnp.concatenate(
                [cache_a_v[j].astype(BF16).reshape(bs, past, KV_W), vb.reshape(bs, ts, KV_W)], 1), s_pad)
            kid_all = _pad_rows(jnp.concatenate(
                [jnp.concatenate([ci, ci], -1), kid.reshape(bs, ts, LANES)], 1), s_pad)
            qpad = lambda a: _pad_rows(a.reshape(bs, ts, a.shape[-1]), TQ).reshape(bs * TQ, a.shape[-1])
            as_ = _dsa(qpad(q), qpad(qm), qpad(qi), qpad(tail), k_all, _key_tiles_t(v_all, tk // 2), kid_all,
                       mkb_s[i], mvb_s[i], s_valid=s_valid, pos0=past, topk=min(TOPK_MAX, s_valid // 4))
            as_ = as_.reshape(bs, TQ, O_W)[:, :ts].reshape(bs * ts, O_W)
            st["aks"].append(heads(kf, bs, ts))
            st["avs"].append(heads(vf, bs, ts))
            st["ais"].append(tail[:, :IDX_DIM].reshape(bs, ts, IDX_DIM))
        else:
            q, kf, vf, kb, vb, qm = _proj(xp, wb[j], None, False)
            tq = LANES
            table = _band_table(rel_bias[j], tq, tq + WIN_B, CHUNK, WIN_B + CHUNK, WIN_B)
            zpad = jnp.zeros((bp, WIN_B, KV_W), BF16)
            ap = _band(q, qm, jnp.concatenate([zpad, kb.reshape(bp, tp, KV_W)], 1),
                       jnp.concatenate([zpad, vb.reshape(bp, tp, KV_W)], 1),
                       table, mkb_p[i], mvb_p[i], tq=tq, kp0=-WIN_B)
            n_keep = min(WIN_B, tp)
            st["bkp"].append(heads(kf, bp, tp)[:, tp - n_keep:])
            st["bvp"].append(heads(vf, bp, tp)[:, tp - n_keep:])

            q, kf, vf, kb, vb, qm = _proj(xs, wb[j], None, False)
            width = -(-(WIN_B + ts) // LANES) * LANES
            table = _band_table(rel_bias[j], ts, width, ts, WIN_B + ts, WIN_B)
            k_all = _pad_rows(jnp.concatenate(
                [cache_b_k[j].astype(BF16).reshape(bs, WIN_B, KV_W), kb.reshape(bs, ts, KV_W)], 1), width)
            v_all = _pad_rows(jnp.concatenate(
                [cache_b_v[j].astype(BF16).reshape(bs, WIN_B, KV_W), vb.reshape(bs, ts, KV_W)], 1), width)
            as_ = _band(q, qm, k_all, v_all, table, mkb_s[i], mvb_s[i], tq=ts, kp0=past - WIN_B)
            st["bks"].append(jnp.concatenate([cache_b_k[j][:, ts:], heads(kf, bs, ts)], 1))
            st["bvs"].append(jnp.concatenate([cache_b_v[j][:, ts:], heads(vf, bs, ts)], 1))

        xp2 = _oproj_ln(ap, wo[i], xp, g[1], b[1], alpha)
        xs2 = _oproj_ln(as_, wo[i], xs, g[1], b[1], alpha)
        xp = _ffn_ln(xp2, w2i[i], w2o[i], g[2], b[2], alpha)
        xs = _ffn_ln(xs2, w2i[i], w2o[i], g[2], b[2], alpha)

    stack = lambda name: jnp.stack(st[name])
    return (xp.reshape(bp, tp, d), xs.reshape(bs, ts, d),
            stack("akp"), stack("avp"), stack("aip"), stack("bkp"), stack("bvp"), mk_p, mv_p,
            stack("aks"), stack("avs"), stack("ais"), stack("bks"), stack("bvs"))
```

```python
import functools

import numpy as np
import jax
import jax.numpy as jnp
from jax import lax
from jax.experimental import pallas as pl
from jax.experimental.pallas import tpu as pltpu

CHUNK = 64
CHUNK_SHIFT = 6
N_HEADS = 12
N_KV_HEADS = 4
HEAD_DIM = 128
GROUP = N_HEADS // N_KV_HEADS
N_MEM_HEADS = 4
IDX_HEADS = 8
IDX_DIM = 64
TOPK_MAX = 256
WIN_B = 8 * CHUNK
REL_CLIP = 128
ROPE_THETA = 10000.0
LN_EPS = 1e-5
Q_W = N_HEADS * HEAD_DIM
KV_W = N_KV_HEADS * HEAD_DIM
MQ_W = N_MEM_HEADS * HEAD_DIM
IDXQ_W = IDX_HEADS * IDX_DIM
O_W = Q_W + MQ_W
PROJ_A = Q_W + 2 * KV_W + MQ_W + IDXQ_W + IDX_DIM + IDX_HEADS
PROJ_B = Q_W + 2 * KV_W + MQ_W
ATTN_SCALE = HEAD_DIM ** -0.5
IDX_W_SCALE = (IDX_HEADS * IDX_DIM) ** -0.5
NEG_INF = -1e30
EXP2_SCALE = ATTN_SCALE * float(np.log2(np.e))

LANES = 128
FFN_TM = 1024
FFN_TF = 512
TQ = 2 * LANES
TQ_SAMPLE = LANES
DSA_TK = 4 * LANES
DSA_HALF = DSA_TK // 2
V_ONES = 16
VMEM_LIMIT = 62 * 1024 * 1024

F32 = jnp.float32
BF16 = jnp.bfloat16
INT_MIN = -(2 ** 31)
NEG_INF_KEY = int(np.array(NEG_INF, np.float32).view(np.int32)) ^ 0x7FFFFFFF


def _dot(a, b):
    return jnp.dot(a, b, preferred_element_type=F32)


def _dot_nt(a, b):
    return lax.dot_general(a, b, (((1,), (1,)), ((), ())), preferred_element_type=F32)


def _params(n_axes):
    return pltpu.CompilerParams(dimension_semantics=("arbitrary",) * n_axes,
                                vmem_limit_bytes=VMEM_LIMIT)


def _resident(block_shape, index_map):
    return pl.BlockSpec(block_shape, index_map, pipeline_mode=pl.Buffered(1))


def _layer_norm(y, g, b):
    mu = jnp.mean(y, axis=-1, keepdims=True)
    d = y - mu
    var = jnp.mean(d * d, axis=-1, keepdims=True)
    return d * lax.rsqrt(var + LN_EPS) * g + b


def _ffn_ln_kernel(x_ref, wg_ref, wu_ref, wo_ref, g_ref, b_ref, o_ref, *, alpha):
    j = pl.program_id(1)

    @pl.when(j == 0)
    def _():
        o_ref[...] = alpha * x_ref[...]

    xb = x_ref[...].astype(BF16)
    gate = _dot(xb, wg_ref[...])
    up = _dot(xb, wu_ref[...])
    h = (gate / (1.0 + jnp.exp(-gate))) * up
    o_ref[...] += _dot(h.astype(BF16), wo_ref[...])

    @pl.when(j == pl.num_programs(1) - 1)
    def _():
        o_ref[...] = _layer_norm(o_ref[...], g_ref[...], b_ref[...])


def _ffn_ln(x, w_in, w_out, layer, g, b, alpha):
    n, d = x.shape
    dff = w_out.shape[1]
    tm = min(FFN_TM, n)
    tf = FFN_TF if dff % FFN_TF == 0 else LANES
    nj = dff // tf
    return pl.pallas_call(
        functools.partial(_ffn_ln_kernel, alpha=alpha),
        grid=(n // tm, nj),
        in_specs=[
            pl.BlockSpec((tm, d), lambda i, j: (i, 0)),
            pl.BlockSpec((None, d, tf), lambda i, j: (layer, 0, j)),
            pl.BlockSpec((None, d, tf), lambda i, j: (layer, 0, j + nj)),
            pl.BlockSpec((None, tf, d), lambda i, j: (layer, j, 0)),
            pl.BlockSpec((1, d), lambda i, j: (0, 0)),
            pl.BlockSpec((1, d), lambda i, j: (0, 0)),
        ],
        out_specs=pl.BlockSpec((tm, d), lambda i, j: (i, 0)),
        out_shape=jax.ShapeDtypeStruct((n, d), F32),
        compiler_params=_params(2),
        name="ffn_ln",
    )(x, w_in, w_in, w_out, g.reshape(1, d), b.reshape(1, d))


def _oproj_ln_kernel(a_ref, w_ref, x_ref, g_ref, b_ref, o_ref, *, alpha):
    y = alpha * x_ref[...] + _dot(a_ref[...], w_ref[...])
    o_ref[...] = _layer_norm(y, g_ref[...], b_ref[...])


def _oproj_ln(a, w, layer, x, g, b, alpha):
    n, d = x.shape
    ow = a.shape[1]
    tm = min(512, n)
    return pl.pallas_call(
        functools.partial(_oproj_ln_kernel, alpha=alpha),
        grid=(n // tm,),
        in_specs=[
            pl.BlockSpec((tm, ow), lambda i: (i, 0)),
            _resident((None, ow, d), lambda i: (layer, 0, 0)),
            pl.BlockSpec((tm, d), lambda i: (i, 0)),
            pl.BlockSpec((1, d), lambda i: (0, 0)),
            pl.BlockSpec((1, d), lambda i: (0, 0)),
        ],
        out_specs=pl.BlockSpec((tm, d), lambda i: (i, 0)),
        out_shape=jax.ShapeDtypeStruct((n, d), F32),
        compiler_params=_params(1),
        name="oproj_ln",
    )(a, w, x, g.reshape(1, d), b.reshape(1, d))


def _mem_kv_kernel(x_ref, w_ref, of_ref, ob_ref):
    y = _dot(x_ref[...].astype(BF16), w_ref[...])
    of_ref[...] = y
    ob_ref[...] = y.astype(BF16)


def _mem_kv(mem, w):
    n, d = mem.shape
    depth, _, c = w.shape
    return pl.pallas_call(
        _mem_kv_kernel,
        grid=(depth,),
        in_specs=[
            _resident((n, d), lambda l: (0, 0)),
            pl.BlockSpec((None, d, c), lambda l: (l, 0, 0)),
        ],
        out_specs=[
            pl.BlockSpec((None, n, c), lambda l: (l, 0, 0)),
            pl.BlockSpec((None, n, c), lambda l: (l, 0, 0)),
        ],
        out_shape=[jax.ShapeDtypeStruct((depth, n, c), F32),
                   jax.ShapeDtypeStruct((depth, n, c), BF16)],
        compiler_params=_params(1),
        name="mem_kv",
    )(mem, w)


def _rope_tables(pos):
    posf = pos.astype(F32)[:, None]

    def cs(half):
        inv_freq = ROPE_THETA ** (-jnp.arange(half, dtype=F32) / half)
        ang = posf * inv_freq[None, :]
        return jnp.cos(ang), jnp.sin(ang)

    c64, s64 = cs(HEAD_DIM // 2)
    c32, s32 = cs(IDX_DIM // 2)
    z32 = jnp.zeros_like(s32)
    cos128 = jnp.concatenate([c64, c64], -1)
    sin128 = jnp.concatenate([-s64, s64], -1)
    cos64 = jnp.concatenate([c32] * 4, -1)
    sa64 = jnp.concatenate([-s32, z32, -s32, z32], -1)
    sb64 = jnp.concatenate([z32, s32, z32, s32], -1)
    lane = jnp.arange(LANES)[None, :]
    first = lane < IDX_DIM
    return jnp.stack([cos128, sin128, cos64, sa64, sb64,
                      jnp.where(first, cos64, 1.0), jnp.where(first, sa64, 0.0),
                      jnp.where(first, sb64, 0.0)])


def _proj_kernel(x_ref, w_ref, *refs, layer_a, emit_vt):
    xb = x_ref[...].astype(BF16)
    vt_ref = None
    if layer_a:
        tab_ref, q_ref, kf_ref, vf_ref, kb_ref, vb_ref, qm_ref, qi_ref, tail_ref, kid_ref = refs[:10]
        if emit_vt:
            vt_ref = refs[10]
    else:
        q_ref, kf_ref, vf_ref, kb_ref, vb_ref, qm_ref = refs

    def rope128(y):
        return y * tab_ref[0] + pltpu.roll(y, HEAD_DIM // 2, axis=1) * tab_ref[1]

    def rope64(y, t0):
        return (y * tab_ref[t0] + pltpu.roll(y, LANES - IDX_DIM // 2, axis=1) * tab_ref[t0 + 1]
                + pltpu.roll(y, IDX_DIM // 2, axis=1) * tab_ref[t0 + 2])

    def section(col0, width, fn):
        step = 512
        for c0 in range(0, width, step):
            cw = min(step, width - c0)
            y = _dot(xb, w_ref[:, col0 + c0:col0 + c0 + cw])
            for s in range(cw // LANES):
                fn(c0 + s * LANES, y[:, s * LANES:(s + 1) * LANES])

    def put(ref, fn=None):
        def f(off, y):
            ref[:, off:off + LANES] = (y if fn is None else fn(y)).astype(ref.dtype)
        return f

    def put2(ref_f, ref_b, fn=None):
        def f(off, y):
            y = y if fn is None else fn(y)
            ref_f[:, off // HEAD_DIM, :] = y
            ref_b[:, off:off + LANES] = y.astype(BF16)
        return f

    put_v = put2(vf_ref, vb_ref)

    def put_v_and_vt(off, y):
        put_v(off, y)
        g = off // HEAD_DIM
        for u in range(y.shape[0] // DSA_HALF):
            vt_ref[u, g, :HEAD_DIM, :] = y[u * DSA_HALF:(u + 1) * DSA_HALF, :].T.astype(BF16)
            vt_ref[u, g, HEAD_DIM:, :] = jnp.ones((V_ONES, DSA_HALF), BF16)

    rq = rope128 if layer_a else None
    section(0, Q_W, put(q_ref, rq))
    section(Q_W, KV_W, put2(kf_ref, kb_ref, rq))
    section(Q_W + KV_W, KV_W, put_v_and_vt if emit_vt else put_v)
    section(Q_W + 2 * KV_W, MQ_W, put(qm_ref))
    if layer_a:
        c = Q_W + 2 * KV_W + MQ_W
        section(c, IDXQ_W, put(qi_ref, lambda y: rope64(y, 2)))

        def tail(off, y):
            t = rope64(y, 5)
            tail_ref[...] = t
            lane = lax.broadcasted_iota(jnp.int32, t.shape, 1)
            dup = jnp.where(lane < IDX_DIM, t, pltpu.roll(t, IDX_DIM, axis=1))
            kid_ref[...] = dup.astype(BF16)
        section(c + IDXQ_W, LANES, tail)


def _proj(x, w, layer, tab, layer_a):
    n, d = x.shape
    pw = w.shape[2]
    tm = min(512, n)
    row = lambda i: (i, 0)
    in_specs = [pl.BlockSpec((tm, d), row), _resident((None, d, pw), lambda i: (layer, 0, 0))]
    args = [x, w]
    kv_state = (N_KV_HEADS, HEAD_DIM)
    outs = [((Q_W,), BF16), (kv_state, F32), (kv_state, F32), ((KV_W,), BF16), ((KV_W,), BF16),
            ((MQ_W,), BF16)]
    if layer_a:
        nt = tab.shape[1] // tm
        in_specs.append(pl.BlockSpec((8, tm, LANES), lambda i: (0, i % nt, 0)))
        args.append(tab)
        outs += [((IDXQ_W,), BF16), ((LANES,), F32), ((LANES,), BF16)]
    emit_vt = layer_a and tm % DSA_HALF == 0
    out_specs = [pl.BlockSpec((tm,) + wd, lambda i, nd=len(wd): (i,) + (0,) * nd) for wd, _ in outs]
    out_shape = [jax.ShapeDtypeStruct((n,) + wd, dt) for wd, dt in outs]
    if emit_vt:
        vt_tile = (N_KV_HEADS, HEAD_DIM + V_ONES, DSA_HALF)
        out_specs.append(pl.BlockSpec((tm // DSA_HALF,) + vt_tile, lambda i: (i, 0, 0, 0)))
        out_shape.append(jax.ShapeDtypeStruct((n // DSA_HALF,) + vt_tile, BF16))
    return pl.pallas_call(
        functools.partial(_proj_kernel, layer_a=layer_a, emit_vt=emit_vt),
        grid=(n // tm,),
        in_specs=in_specs,
        out_specs=out_specs,
        out_shape=out_shape,
        compiler_params=_params(1),
        name="proj_a" if layer_a else "proj_b",
    )(*args)


def _stack_heads(q_ref, g):
    return jnp.concatenate(
        [q_ref[:, (GROUP * g + r) * HEAD_DIM:(GROUP * g + r + 1) * HEAD_DIM] for r in range(GROUP)],
        axis=0)


def _unstack_heads(o_ref, g, o, tq):
    for r in range(GROUP):
        h = GROUP * g + r
        o_ref[:, h * HEAD_DIM:(h + 1) * HEAD_DIM] = o[r * tq:(r + 1) * tq].astype(o_ref.dtype)


def _mem_attend(qm_ref, mk_ref, mv_ref, o_ref):
    for hm in range(N_MEM_HEADS):
        sl = slice(hm * HEAD_DIM, (hm + 1) * HEAD_DIM)
        s = _dot_nt(qm_ref[:, sl], mk_ref[:, sl]) * ATTN_SCALE
        m = jnp.max(s, axis=1, keepdims=True)
        p = jnp.exp(s - m)
        l = jnp.sum(p, axis=1, keepdims=True)
        om = _dot(p.astype(BF16), mv_ref[:, sl]) / l
        o_ref[:, Q_W + hm * HEAD_DIM:Q_W + (hm + 1) * HEAD_DIM] = om.astype(o_ref.dtype)


def _sort_key(x):
    bits = lax.bitcast_convert_type(x, jnp.int32)
    return bits ^ ((bits >> 31) & 0x7FFFFFFF)


def _dsa_kernel(q_ref, qm_ref, qi_ref, tail_ref, k_ref, vt_ref, kid_ref, mk_ref, mv_ref,
                o_ref, keys_ref, hi_ref, lo_ref, sa_ref, sb_ref, *, tq, tk, s_valid, pos0, topk):
    i = pl.program_id(1)
    row0 = pos0 + i * tq
    qpos = row0 + lax.broadcasted_iota(jnp.int32, (1, tq), 1)
    lim = jnp.minimum(((qpos >> CHUNK_SHIFT) + 1) * CHUNK, s_valid)
    lim_max = jnp.minimum((((row0 + tq - 1) >> CHUNK_SHIFT) + 1) * CHUNK, s_valid)
    n_kt = (lim_max + tk - 1) // tk

    lane = lax.broadcasted_iota(jnp.int32, (tq, LANES), 1)
    pairs = []
    for p in range(IDX_HEADS // 2):
        qp = qi_ref[:, p * LANES:(p + 1) * LANES]
        zero = jnp.zeros_like(qp)
        pairs.append(jnp.concatenate(
            [jnp.where(lane < IDX_DIM, qp, zero), jnp.where(lane >= IDX_DIM, qp, zero)], axis=0))
    wt = tail_ref[...].T[IDX_DIM:IDX_DIM + IDX_HEADS, :] * IDX_W_SCALE
    krow = lax.broadcasted_iota(jnp.int32, (tk, tq), 0)

    def idx_body(kt, carry):
        kid = kid_ref[pl.ds(pl.multiple_of(kt * tk, tk), tk), :]
        acc = jnp.zeros((tk, tq), F32)
        for p in range(IDX_HEADS // 2):
            rel = jnp.maximum(_dot_nt(kid, pairs[p]), 0.0)
            acc = acc + wt[2 * p:2 * p + 1, :] * rel[:, :tq]
            acc = acc + wt[2 * p + 1:2 * p + 2, :] * rel[:, tq:]
        score = jnp.where(krow + kt * tk < lim, acc, NEG_INF)
        key = _sort_key(score)
        keys_ref[kt] = key
        hi_ref[kt] = (key >> 16).astype(jnp.int16)
        lo_ref[kt] = (key ^ 0x8000).astype(jnp.int16)
        return carry

    lax.fori_loop(0, n_kt, idx_body, 0)

    n_part = 4
    sub16 = 16

    def count16(ref, pred, cand):
        candb = jnp.broadcast_to(cand.astype(jnp.int16), (sub16, tq))

        def body(kt, cnts):
            xx = ref[kt]
            cnts = list(cnts)
            for r in range(tk // sub16):
                hit = pred(xx[r * sub16:(r + 1) * sub16, :], candb).astype(jnp.int16)
                cnts[r % n_part] = cnts[r % n_part] + hit
            return tuple(cnts)

        zero = jnp.zeros((sub16, tq), jnp.int16)
        cnts = lax.fori_loop(0, n_kt, body, (zero,) * n_part)
        total = sum(c.astype(F32) for c in cnts)
        return jnp.sum(total, axis=0, keepdims=True)

    ge = lambda a, b: a >= b
    gt = lambda a, b: a > b
    i16_min = -(2 ** 15)

    def search16(ref, need):
        zeros = jnp.zeros((1, tq), jnp.int32)
        thr0 = jnp.where(count16(ref, ge, zeros) >= need, zeros, jnp.full_like(zeros, i16_min))

        def step(it, thr):
            cand = thr | lax.shift_left(jnp.int32(1), 14 - it)
            return jnp.where(count16(ref, ge, cand) >= need, cand, thr)

        return lax.fori_loop(0, 15, step, thr0)

    need = jnp.full((1, tq), float(topk), F32)
    thr_hi = search16(hi_ref, need)
    need_lo = need - count16(hi_ref, gt, thr_hi)

    thr_hib = jnp.broadcast_to(thr_hi.astype(jnp.int16), (tk, tq))

    def keep_lo(kt, carry):
        lo_ref[kt] = jnp.where(hi_ref[kt] == thr_hib, lo_ref[kt], jnp.int16(i16_min))
        return carry

    lax.fori_loop(0, n_kt, keep_lo, 0)
    thr_lo = search16(lo_ref, need_lo)
    thr = (thr_hi << 16) | ((thr_lo ^ 0x8000) & 0xFFFF)
    thr_b = jnp.broadcast_to(thr, (tk, tq))

    def count32(pred):
        def body(kt, cnt):
            hit = jnp.where(pred(keys_ref[kt], kt * tk), 1, 0)
            for r in range(tk // 8):
                cnt = cnt + hit[r * 8:(r + 1) * 8, :]
            return cnt
        cnt = lax.fori_loop(0, n_kt, body, jnp.zeros((8, tq), jnp.int32))
        return jnp.sum(cnt.astype(F32), axis=0, keepdims=True)

    surplus = jnp.where(thr > NEG_INF_KEY, count32(lambda kk, k0: kk >= thr_b) - need, 0.0)

    @pl.when(jnp.max(surplus) > 0.0)
    def _():
        keep = count32(lambda kk, k0: kk == thr_b) - surplus
        idx_bits = (keys_ref.shape[0] * tk).bit_length()

        def step(it, bound):
            cand = bound | lax.shift_left(jnp.int32(1), idx_bits - 1 - it)
            cand_b = jnp.broadcast_to(cand, (tk, tq))
            below = count32(lambda kk, k0: jnp.logical_and(kk == thr_b, krow + k0 < cand_b))
            return jnp.where(below <= keep, cand, bound)

        bound = lax.fori_loop(0, idx_bits, step, jnp.zeros((1, tq), jnp.int32))
        bound_b = jnp.broadcast_to(bound, (tk, tq))

        def demote(kt, carry):
            kk = keys_ref[kt]
            drop = jnp.logical_and(kk == thr_b, krow + kt * tk >= bound_b)
            keys_ref[kt] = jnp.where(drop, thr_b - 1, kk)
            return carry

        lax.fori_loop(0, n_kt, demote, 0)

    thr = jnp.maximum(thr, NEG_INF_KEY + 1)

    q3 = [_stack_heads(q_ref, g) for g in range(N_KV_HEADS)]

    half = tk // 2
    thr_h = jnp.broadcast_to(thr, (half, tq))

    def qk_stage(kt, sub, buf_ref):
        off = pl.multiple_of(kt * tk + sub * half, half)
        kk = keys_ref[kt, sub * half:(sub + 1) * half, :]
        bias = jnp.where(kk >= thr_h, 0.0, NEG_INF)
        bias = jnp.concatenate([bias] * GROUP, axis=1)
        for g in range(N_KV_HEADS):
            hs = slice(g * HEAD_DIM, (g + 1) * HEAD_DIM)
            buf_ref[g] = _dot_nt(k_ref[pl.ds(off, half), hs], q3[g]) + bias

    def softmax_stage(kt, sub, buf_ref, carry):
        out = []
        for g in range(N_KV_HEADS):
            m, acc = carry[g]
            m_new = jnp.maximum(m, jnp.max(buf_ref[g], axis=0, keepdims=True))
            a = jnp.exp2((m - m_new) * EXP2_SCALE)
            p = jnp.exp2((buf_ref[g] - m_new) * EXP2_SCALE)
            acc = a * acc + _dot(vt_ref[2 * kt + sub, g], p.astype(BF16))
            out.append((m_new, acc))
        return tuple(out)

    def att_body(kt, carry):
        qk_stage(kt, 1, sb_ref)
        carry = softmax_stage(kt, 0, sa_ref, carry)
        qk_stage(jnp.minimum(kt + 1, n_kt - 1), 0, sa_ref)
        return softmax_stage(kt, 1, sb_ref, carry)

    init = tuple((jnp.full((1, GROUP * tq), NEG_INF, F32),
                  jnp.zeros((HEAD_DIM + V_ONES, GROUP * tq), F32)) for _ in range(N_KV_HEADS))
    qk_stage(0, 0, sa_ref)
    res = lax.fori_loop(0, n_kt, att_body, init)
    for g in range(N_KV_HEADS):
        acc = res[g][1]
        o_t = acc[:HEAD_DIM] / acc[HEAD_DIM:HEAD_DIM + 1]
        for r in range(GROUP):
            h = GROUP * g + r
            o_ref[:, h * HEAD_DIM:(h + 1) * HEAD_DIM] = o_t[:, r * tq:(r + 1) * tq].T.astype(o_ref.dtype)

    _mem_attend(qm_ref, mk_ref, mv_ref, o_ref)


def _dsa(q, qm, qi, tail, k, vt, kid, mk, mv, *, tq, s_valid, pos0, topk):
    n = q.shape[0]
    nb, n_half, _, vrows, half = vt.shape
    n_kt, tk = n_half // 2, 2 * half
    s_pad = k.shape[1]
    nq = n // nb // tq
    n_mem = mk.shape[0] // nb
    row = lambda b, i: (b * nq + i, 0)
    per_b = lambda b, i: (b, 0, 0)
    return pl.pallas_call(
        functools.partial(_dsa_kernel, tq=tq, tk=tk, s_valid=s_valid, pos0=pos0, topk=topk),
        grid=(nb, nq),
        in_specs=[
            pl.BlockSpec((tq, Q_W), row),
            pl.BlockSpec((tq, MQ_W), row),
            pl.BlockSpec((tq, IDXQ_W), row),
            pl.BlockSpec((tq, LANES), row),
            _resident((None, s_pad, KV_W), per_b),
            _resident((None, n_half, N_KV_HEADS, vrows, half), lambda b, i: (b, 0, 0, 0, 0)),
            _resident((None, s_pad, LANES), per_b),
            pl.BlockSpec((n_mem, MQ_W), lambda b, i: (b, 0)),
            pl.BlockSpec((n_mem, MQ_W), lambda b, i: (b, 0)),
        ],
        out_specs=pl.BlockSpec((tq, O_W), row),
        out_shape=jax.ShapeDtypeStruct((n, O_W), BF16),
        scratch_shapes=[pltpu.VMEM((n_kt, tk, tq), jnp.int32),
                        pltpu.VMEM((n_kt, tk, tq), jnp.int16),
                        pltpu.VMEM((n_kt, tk, tq), jnp.int16),
                        pltpu.VMEM((N_KV_HEADS, half, GROUP * tq), F32),
                        pltpu.VMEM((N_KV_HEADS, half, GROUP * tq), F32)],
        compiler_params=_params(2),
        name="dsa_attend",
    )(q, qm, qi, tail, k, vt, kid, mk, mv)


def _key_tiles_t(v, tk):
    nb, s, _ = v.shape
    vt = v.reshape(nb, s // tk, tk, N_KV_HEADS, HEAD_DIM).transpose(0, 1, 3, 4, 2)
    ones = jnp.ones((nb, s // tk, N_KV_HEADS, V_ONES, tk), v.dtype)
    return jnp.concatenate([vt, ones], axis=3)


def _band_table_kernel(rb_ref, o_ref, *, qc, band, prefix):
    h = pl.program_id(0)
    shape = o_ref.shape
    r = lax.broadcasted_iota(jnp.int32, shape, 0)
    w = lax.broadcasted_iota(jnp.int32, shape, 1)
    idx = jnp.clip(r - w + prefix, -REL_CLIP, REL_CLIP) + REL_CLIP
    lo = (r // qc) * qc
    inband = jnp.logical_and(w >= lo, w < lo + band)

    def body(t, acc):
        return jnp.where(idx == t, rb_ref[h, t], acc)

    acc = lax.fori_loop(0, 2 * REL_CLIP + 1, body, jnp.zeros(shape, F32))
    o_ref[...] = jnp.where(inband, acc, NEG_INF)


def _band_table(rel_bias, rows, width, qc, band, prefix):
    nh = rel_bias.shape[0]
    return pl.pallas_call(
        functools.partial(_band_table_kernel, qc=qc, band=band, prefix=prefix),
        grid=(nh,),
        in_specs=[pl.BlockSpec(memory_space=pltpu.SMEM)],
        out_specs=pl.BlockSpec((None, rows, width), lambda h: (h, 0, 0)),
        out_shape=jax.ShapeDtypeStruct((nh, rows, width), F32),
        compiler_params=_params(1),
        name="band_table",
    )(rel_bias)


def _band_kernel(q_ref, qm_ref, k_ref, v_ref, tab_ref, mk_ref, mv_ref, o_ref, *, tq, width, kp0):
    i = pl.program_id(1)
    off = pl.multiple_of(i * tq, tq)
    kp_ok = (kp0 + i * tq + lax.broadcasted_iota(jnp.int32, (GROUP * tq, width), 1)) >= 0
    for g in range(N_KV_HEADS):
        hs = slice(g * HEAD_DIM, (g + 1) * HEAD_DIM)
        q3 = _stack_heads(q_ref, g)
        s = _dot_nt(q3, k_ref[pl.ds(off, width), hs]) * ATTN_SCALE
        s = s + jnp.concatenate([tab_ref[GROUP * g + r] for r in range(GROUP)], axis=0)
        s = jnp.where(kp_ok, s, NEG_INF)
        m = jnp.max(s, axis=1, keepdims=True)
        p = jnp.exp(s - m)
        l = jnp.sum(p, axis=1, keepdims=True)
        o = _dot(p.astype(BF16), v_ref[pl.ds(off, width), hs]) / l
        _unstack_heads(o_ref, g, o, tq)
    _mem_attend(qm_ref, mk_ref, mv_ref, o_ref)


def _band(q, qm, k_all, v_all, table, mk, mv, *, tq, kp0):
    n = q.shape[0]
    nb, l_pad, _ = k_all.shape
    nq = n // nb // tq
    n_mem = mk.shape[0] // nb
    width = table.shape[2]
    row = lambda b, i: (b * nq + i, 0)
    per_b = lambda b, i: (b, 0, 0)
    return pl.pallas_call(
        functools.partial(_band_kernel, tq=tq, width=width, kp0=kp0),
        grid=(nb, nq),
        in_specs=[
            pl.BlockSpec((tq, Q_W), row),
            pl.BlockSpec((tq, MQ_W), row),
            _resident((None, l_pad, KV_W), per_b),
            _resident((None, l_pad, KV_W), per_b),
            _resident(table.shape, lambda b, i: (0, 0, 0)),
            pl.BlockSpec((n_mem, MQ_W), lambda b, i: (b, 0)),
            pl.BlockSpec((n_mem, MQ_W), lambda b, i: (b, 0)),
        ],
        out_specs=pl.BlockSpec((tq, O_W), row),
        out_shape=jax.ShapeDtypeStruct((n, O_W), BF16),
        compiler_params=_params(2),
        name="band_attend",
    )(q, qm, k_all, v_all, table, mk, mv)


def _pad_rows(a, rows):
    return jnp.pad(a, ((0, 0), (0, rows - a.shape[1]), (0, 0)))


def kernel(x_prompt, x_sample, mem_prompt, cache_a_k, cache_a_v, cache_a_kidx, cache_b_k, cache_b_v,
           cache_mem_k, cache_mem_v, w_in_a, w_in_b, w_o, rel_bias, w_mem_kv,
           w_ffn1_in, w_ffn1_out, w_ffn2_in, w_ffn2_out, ln_g, ln_b):
    depth = w_o.shape[0]
    alpha = (2 * depth) ** 0.25
    bp, tp, d = x_prompt.shape
    bs, ts, _ = x_sample.shape
    past = cache_a_k.shape[2]
    n_mem = mem_prompt.shape[1]
    assert tp % DSA_TK == 0 and ts % 16 == 0 and ts <= CHUNK
    assert cache_b_k.shape[2] == WIN_B and tp >= WIN_B

    pa_pad = -(-PROJ_A // LANES) * LANES
    wa = jnp.pad(w_in_a, ((0, 0), (0, 0), (0, pa_pad - PROJ_A))).astype(BF16)
    wb, wo = w_in_b.astype(BF16), w_o.astype(BF16)
    w1i, w1o = w_ffn1_in.astype(BF16), (0.5 * w_ffn1_out).astype(BF16)
    w2i, w2o = w_ffn2_in.astype(BF16), (0.5 * w_ffn2_out).astype(BF16)

    memf, memb = _mem_kv(mem_prompt.reshape(bp * n_mem, d), w_mem_kv.astype(BF16))
    mk_p = memf[..., :MQ_W].reshape(depth, bp, n_mem, N_MEM_HEADS, HEAD_DIM)
    mv_p = memf[..., MQ_W:].reshape(depth, bp, n_mem, N_MEM_HEADS, HEAD_DIM)
    mkb_p, mvb_p = memb[..., :MQ_W], memb[..., MQ_W:]
    mkb_s = cache_mem_k.astype(BF16).reshape(depth, bs * n_mem, MQ_W)
    mvb_s = cache_mem_v.astype(BF16).reshape(depth, bs * n_mem, MQ_W)

    tab_p = _rope_tables(jnp.arange(tp))
    tab_s = jnp.tile(_rope_tables(past + jnp.arange(ts)), (1, bs, 1))

    xp = x_prompt.reshape(bp * tp, d)
    xs = x_sample.reshape(bs * ts, d)
    st = {k: [] for k in ("akp", "avp", "aip", "bkp", "bvp", "aks", "avs", "ais", "bks", "bvs")}
    heads = lambda a, b, t: a.reshape(b, t, N_KV_HEADS, HEAD_DIM)

    for i in range(depth):
        j = i // 2
        g, b = ln_g[i], ln_b[i]
        xp = _ffn_ln(xp, w1i, w1o, i, g[0], b[0], alpha)
        xs = _ffn_ln(xs, w1i, w1o, i, g[0], b[0], alpha)
        if i % 2 == 0:
            q, kf, vf, kb, vb, qm, qi, tail, kid, vt = _proj(xp, wa, j, tab_p, True)
            ap = _dsa(q, qm, qi, tail, kb.reshape(bp, tp, KV_W),
                      vt.reshape((bp, tp // DSA_HALF) + vt.shape[1:]),
                      kid.reshape(bp, tp, LANES), mkb_p[i], mvb_p[i],
                      tq=min(TQ, tp), s_valid=tp, pos0=0, topk=min(TOPK_MAX, tp // 4))
            st["akp"].append(heads(kf, bp, tp))
            st["avp"].append(heads(vf, bp, tp))
            st["aip"].append(tail[:, :IDX_DIM].reshape(bp, tp, IDX_DIM))

            q, kf, vf, kb, vb, qm, qi, tail, kid = _proj(xs, wa, j, tab_s, True)
            tk = DSA_TK
            s_valid = past + ts
            s_pad = -(-s_valid // tk) * tk
            ci = cache_a_kidx[j].astype(BF16)
            k_all = _pad_rows(jnp.concatenate(
                [cache_a_k[j].astype(BF16).reshape(bs, past, KV_W), kb.reshape(bs, ts, KV_W)], 1), s_pad)
            v_all = _pad_rows(jnp.concatenate(
                [cache_a_v[j].astype(BF16).reshape(bs, past, KV_W), vb.reshape(bs, ts, KV_W)], 1), s_pad)
            kid_all = _pad_rows(jnp.concatenate(
                [jnp.concatenate([ci, ci], -1), kid.reshape(bs, ts, LANES)], 1), s_pad)
            tq = TQ_SAMPLE
            qpad = lambda a: _pad_rows(a.reshape(bs, ts, a.shape[-1]), tq).reshape(bs * tq, a.shape[-1])
            as_ = _dsa(qpad(q), qpad(qm), qpad(qi), qpad(tail), k_all, _key_tiles_t(v_all, tk // 2), kid_all,
                       mkb_s[i], mvb_s[i], tq=tq, s_valid=s_valid, pos0=past,
                       topk=min(TOPK_MAX, s_valid // 4))
            as_ = as_.reshape(bs, tq, O_W)[:, :ts].reshape(bs * ts, O_W)
            st["aks"].append(heads(kf, bs, ts))
            st["avs"].append(heads(vf, bs, ts))
            st["ais"].append(tail[:, :IDX_DIM].reshape(bs, ts, IDX_DIM))
        else:
            q, kf, vf, kb, vb, qm = _proj(xp, wb, j, None, False)
            tq = LANES
            table = _band_table(rel_bias[j], tq, tq + WIN_B, CHUNK, WIN_B + CHUNK, WIN_B)
            zpad = jnp.zeros((bp, WIN_B, KV_W), BF16)
            ap = _band(q, qm, jnp.concatenate([zpad, kb.reshape(bp, tp, KV_W)], 1),
                       jnp.concatenate([zpad, vb.reshape(bp, tp, KV_W)], 1),
                       table, mkb_p[i], mvb_p[i], tq=tq, kp0=-WIN_B)
            n_keep = min(WIN_B, tp)
            st["bkp"].append(heads(kf, bp, tp)[:, tp - n_keep:])
            st["bvp"].append(heads(vf, bp, tp)[:, tp - n_keep:])

            q, kf, vf, kb, vb, qm = _proj(xs, wb, j, None, False)
            width = -(-(WIN_B + ts) // LANES) * LANES
            table = _band_table(rel_bias[j], ts, width, ts, WIN_B + ts, WIN_B)
            k_all = _pad_rows(jnp.concatenate(
                [cache_b_k[j].astype(BF16).reshape(bs, WIN_B, KV_W), kb.reshape(bs, ts, KV_W)], 1), width)
            v_all = _pad_rows(jnp.concatenate(
                [cache_b_v[j].astype(BF16).reshape(bs, WIN_B, KV_W), vb.reshape(bs, ts, KV_W)], 1), width)
            as_ = _band(q, qm, k_all, v_all, table, mkb_s[i], mvb_s[i], tq=ts, kp0=past - WIN_B)
            st["bks"].append(jnp.concatenate([cache_b_k[j][:, ts:], heads(kf, bs, ts)], 1))
            st["bvs"].append(jnp.concatenate([cache_b_v[j][:, ts:], heads(vf, bs, ts)], 1))

        xp2 = _oproj_ln(ap, wo, i, xp, g[1], b[1], alpha)
        xs2 = _oproj_ln(as_, wo, i, xs, g[1], b[1], alpha)
        xp = _ffn_ln(xp2, w2i, w2o, i, g[2], b[2], alpha)
        xs = _ffn_ln(xs2, w2i, w2o, i, g[2], b[2], alpha)

    stack = lambda name: jnp.stack(st[name])
    return (xp.reshape(bp, tp, d), xs.reshape(bs, ts, d),
            stack("akp"), stack("avp"), stack("aip"), stack("bkp"), stack("bvp"), mk_p, mv_p,
            stack("aks"), stack("avs"), stack("ais"), stack("bks"), stack("bvs"))
```

```python
import functools

import numpy as np
import jax
import jax.numpy as jnp
from jax import lax
from jax.experimental import pallas as pl
from jax.experimental.pallas import tpu as pltpu

CHUNK = 64
CHUNK_SHIFT = 6
N_HEADS = 12
N_KV_HEADS = 4
HEAD_DIM = 128
GROUP = N_HEADS // N_KV_HEADS
N_MEM_HEADS = 4
IDX_HEADS = 8
IDX_DIM = 64
TOPK_MAX = 256
WIN_B = 8 * CHUNK
REL_CLIP = 128
ROPE_THETA = 10000.0
LN_EPS = 1e-5
Q_W = N_HEADS * HEAD_DIM
KV_W = N_KV_HEADS * HEAD_DIM
MQ_W = N_MEM_HEADS * HEAD_DIM
IDXQ_W = IDX_HEADS * IDX_DIM
O_W = Q_W + MQ_W
PROJ_A = Q_W + 2 * KV_W + MQ_W + IDXQ_W + IDX_DIM + IDX_HEADS
PROJ_B = Q_W + 2 * KV_W + MQ_W
ATTN_SCALE = HEAD_DIM ** -0.5
IDX_W_SCALE = (IDX_HEADS * IDX_DIM) ** -0.5
NEG_INF = -1e30
EXP2_SCALE = ATTN_SCALE * float(np.log2(np.e))

LANES = 128
FFN_TM = 1024
FFN_TF = 512
TQ = 2 * LANES
TQ_SAMPLE = LANES
DSA_TK = 4 * LANES
DSA_HALF = DSA_TK // 2
V_ONES = 16
VMEM_LIMIT = 62 * 1024 * 1024

F32 = jnp.float32
BF16 = jnp.bfloat16
INT_MIN = -(2 ** 31)
NEG_INF_KEY = int(np.array(NEG_INF, np.float32).view(np.int32)) ^ 0x7FFFFFFF


def _dot(a, b):
    return jnp.dot(a, b, preferred_element_type=F32)


def _dot_nt(a, b):
    return lax.dot_general(a, b, (((1,), (1,)), ((), ())), preferred_element_type=F32)


def _params(n_axes):
    return pltpu.CompilerParams(dimension_semantics=("arbitrary",) * n_axes,
                                vmem_limit_bytes=VMEM_LIMIT)


def _resident(block_shape, index_map):
    return pl.BlockSpec(block_shape, index_map, pipeline_mode=pl.Buffered(1))


def _layer_norm(y, g, b):
    mu = jnp.mean(y, axis=-1, keepdims=True)
    d = y - mu
    var = jnp.mean(d * d, axis=-1, keepdims=True)
    return d * lax.rsqrt(var + LN_EPS) * g + b


def _ffn_ln_kernel(x_ref, wg_ref, wu_ref, wo_ref, g_ref, b_ref, o_ref, *, alpha):
    j = pl.program_id(1)

    @pl.when(j == 0)
    def _():
        o_ref[...] = alpha * x_ref[...]

    xb = x_ref[...].astype(BF16)
    gate = _dot(xb, wg_ref[...])
    up = _dot(xb, wu_ref[...])
    h = (gate / (1.0 + jnp.exp(-gate))) * up
    o_ref[...] += _dot(h.astype(BF16), wo_ref[...])

    @pl.when(j == pl.num_programs(1) - 1)
    def _():
        o_ref[...] = _layer_norm(o_ref[...], g_ref[...], b_ref[...])


def _ffn_ln(x, w_in, w_out, layer, g, b, alpha):
    n, d = x.shape
    dff = w_out.shape[1]
    tm = min(FFN_TM, n)
    tf = FFN_TF if dff % FFN_TF == 0 else LANES
    nj = dff // tf
    return pl.pallas_call(
        functools.partial(_ffn_ln_kernel, alpha=alpha),
        grid=(n // tm, nj),
        in_specs=[
            pl.BlockSpec((tm, d), lambda i, j: (i, 0)),
            pl.BlockSpec((None, d, tf), lambda i, j: (layer, 0, j)),
            pl.BlockSpec((None, d, tf), lambda i, j: (layer, 0, j + nj)),
            pl.BlockSpec((None, tf, d), lambda i, j: (layer, j, 0)),
            pl.BlockSpec((1, d), lambda i, j: (0, 0)),
            pl.BlockSpec((1, d), lambda i, j: (0, 0)),
        ],
        out_specs=pl.BlockSpec((tm, d), lambda i, j: (i, 0)),
        out_shape=jax.ShapeDtypeStruct((n, d), F32),
        compiler_params=_params(2),
        name="ffn_ln",
    )(x, w_in, w_in, w_out, g.reshape(1, d), b.reshape(1, d))


def _oproj_ln_kernel(a_ref, w_ref, x_ref, g_ref, b_ref, o_ref, *, alpha):
    y = alpha * x_ref[...] + _dot(a_ref[...], w_ref[...])
    o_ref[...] = _layer_norm(y, g_ref[...], b_ref[...])


def _oproj_ln(a, w, layer, x, g, b, alpha):
    n, d = x.shape
    ow = a.shape[1]
    tm = min(512, n)
    return pl.pallas_call(
        functools.partial(_oproj_ln_kernel, alpha=alpha),
        grid=(n // tm,),
        in_specs=[
            pl.BlockSpec((tm, ow), lambda i: (i, 0)),
            _resident((None, ow, d), lambda i: (layer, 0, 0)),
            pl.BlockSpec((tm, d), lambda i: (i, 0)),
            pl.BlockSpec((1, d), lambda i: (0, 0)),
            pl.BlockSpec((1, d), lambda i: (0, 0)),
        ],
        out_specs=pl.BlockSpec((tm, d), lambda i: (i, 0)),
        out_shape=jax.ShapeDtypeStruct((n, d), F32),
        compiler_params=_params(1),
        name="oproj_ln",
    )(a, w, x, g.reshape(1, d), b.reshape(1, d))


def _mem_kv_kernel(x_ref, w_ref, of_ref, ob_ref):
    y = _dot(x_ref[...].astype(BF16), w_ref[...])
    of_ref[...] = y
    ob_ref[...] = y.astype(BF16)


def _mem_kv(mem, w):
    n, d = mem.shape
    depth, _, c = w.shape
    return pl.pallas_call(
        _mem_kv_kernel,
        grid=(depth,),
        in_specs=[
            _resident((n, d), lambda l: (0, 0)),
            pl.BlockSpec((None, d, c), lambda l: (l, 0, 0)),
        ],
        out_specs=[
            pl.BlockSpec((None, n, c), lambda l: (l, 0, 0)),
            pl.BlockSpec((None, n, c), lambda l: (l, 0, 0)),
        ],
        out_shape=[jax.ShapeDtypeStruct((depth, n, c), F32),
                   jax.ShapeDtypeStruct((depth, n, c), BF16)],
        compiler_params=_params(1),
        name="mem_kv",
    )(mem, w)


def _rope_tables(pos):
    posf = pos.astype(F32)[:, None]

    def cs(half):
        inv_freq = ROPE_THETA ** (-jnp.arange(half, dtype=F32) / half)
        ang = posf * inv_freq[None, :]
        return jnp.cos(ang), jnp.sin(ang)

    c64, s64 = cs(HEAD_DIM // 2)
    c32, s32 = cs(IDX_DIM // 2)
    z32 = jnp.zeros_like(s32)
    cos128 = jnp.concatenate([c64, c64], -1)
    sin128 = jnp.concatenate([-s64, s64], -1)
    cos64 = jnp.concatenate([c32] * 4, -1)
    sa64 = jnp.concatenate([-s32, z32, -s32, z32], -1)
    sb64 = jnp.concatenate([z32, s32, z32, s32], -1)
    lane = jnp.arange(LANES)[None, :]
    first = lane < IDX_DIM
    return jnp.stack([cos128, sin128, cos64, sa64, sb64,
                      jnp.where(first, cos64, 1.0), jnp.where(first, sa64, 0.0),
                      jnp.where(first, sb64, 0.0)])


def _proj_kernel(x_ref, w_ref, *refs, layer_a, emit_vt):
    xb = x_ref[...].astype(BF16)
    vt_ref = None
    if layer_a:
        tab_ref, q_ref, kf_ref, vf_ref, kb_ref, vb_ref, qm_ref, qi_ref, tail_ref, kid_ref = refs[:10]
        if emit_vt:
            vt_ref = refs[10]
    else:
        q_ref, kf_ref, vf_ref, kb_ref, vb_ref, qm_ref = refs

    def rope128(y):
        return y * tab_ref[0] + pltpu.roll(y, HEAD_DIM // 2, axis=1) * tab_ref[1]

    def rope64(y, t0):
        return (y * tab_ref[t0] + pltpu.roll(y, LANES - IDX_DIM // 2, axis=1) * tab_ref[t0 + 1]
                + pltpu.roll(y, IDX_DIM // 2, axis=1) * tab_ref[t0 + 2])

    def section(col0, width, fn):
        step = 512
        for c0 in range(0, width, step):
            cw = min(step, width - c0)
            y = _dot(xb, w_ref[:, col0 + c0:col0 + c0 + cw])
            for s in range(cw // LANES):
                fn(c0 + s * LANES, y[:, s * LANES:(s + 1) * LANES])

    def put(ref, fn=None):
        def f(off, y):
            ref[:, off:off + LANES] = (y if fn is None else fn(y)).astype(ref.dtype)
        return f

    def put2(ref_f, ref_b, fn=None):
        def f(off, y):
            y = y if fn is None else fn(y)
            ref_f[:, off // HEAD_DIM, :] = y
            ref_b[:, off:off + LANES] = y.astype(BF16)
        return f

    put_v = put2(vf_ref, vb_ref)

    def put_v_and_vt(off, y):
        put_v(off, y)
        g = off // HEAD_DIM
        for u in range(y.shape[0] // DSA_HALF):
            vt_ref[u, g, :HEAD_DIM, :] = y[u * DSA_HALF:(u + 1) * DSA_HALF, :].T.astype(BF16)
            vt_ref[u, g, HEAD_DIM:, :] = jnp.ones((V_ONES, DSA_HALF), BF16)

    rq = rope128 if layer_a else None
    section(0, Q_W, put(q_ref, rq))
    section(Q_W, KV_W, put2(kf_ref, kb_ref, rq))
    section(Q_W + KV_W, KV_W, put_v_and_vt if emit_vt else put_v)
    section(Q_W + 2 * KV_W, MQ_W, put(qm_ref))
    if layer_a:
        c = Q_W + 2 * KV_W + MQ_W
        section(c, IDXQ_W, put(qi_ref, lambda y: rope64(y, 2)))

        def tail(off, y):
            t = rope64(y, 5)
            tail_ref[...] = t
            lane = lax.broadcasted_iota(jnp.int32, t.shape, 1)
            dup = jnp.where(lane < IDX_DIM, t, pltpu.roll(t, IDX_DIM, axis=1))
            kid_ref[...] = dup.astype(BF16)
        section(c + IDXQ_W, LANES, tail)


def _proj(x, w, layer, tab, layer_a):
    n, d = x.shape
    pw = w.shape[2]
    tm = min(512, n)
    row = lambda i: (i, 0)
    in_specs = [pl.BlockSpec((tm, d), row), _resident((None, d, pw), lambda i: (layer, 0, 0))]
    args = [x, w]
    kv_state = (N_KV_HEADS, HEAD_DIM)
    outs = [((Q_W,), BF16), (kv_state, F32), (kv_state, F32), ((KV_W,), BF16), ((KV_W,), BF16),
            ((MQ_W,), BF16)]
    if layer_a:
        nt = tab.shape[1] // tm
        in_specs.append(pl.BlockSpec((8, tm, LANES), lambda i: (0, i % nt, 0)))
        args.append(tab)
        outs += [((IDXQ_W,), BF16), ((LANES,), F32), ((LANES,), BF16)]
    emit_vt = layer_a and tm % DSA_HALF == 0
    out_specs = [pl.BlockSpec((tm,) + wd, lambda i, nd=len(wd): (i,) + (0,) * nd) for wd, _ in outs]
    out_shape = [jax.ShapeDtypeStruct((n,) + wd, dt) for wd, dt in outs]
    if emit_vt:
        vt_tile = (N_KV_HEADS, HEAD_DIM + V_ONES, DSA_HALF)
        out_specs.append(pl.BlockSpec((tm // DSA_HALF,) + vt_tile, lambda i: (i, 0, 0, 0)))
        out_shape.append(jax.ShapeDtypeStruct((n // DSA_HALF,) + vt_tile, BF16))
    return pl.pallas_call(
        functools.partial(_proj_kernel, layer_a=layer_a, emit_vt=emit_vt),
        grid=(n // tm,),
        in_specs=in_specs,
        out_specs=out_specs,
        out_shape=out_shape,
        compiler_params=_params(1),
        name="proj_a" if layer_a else "proj_b",
    )(*args)


def _stack_heads(q_ref, g):
    return jnp.concatenate(
        [q_ref[:, (GROUP * g + r) * HEAD_DIM:(GROUP * g + r + 1) * HEAD_DIM] for r in range(GROUP)],
        axis=0)


def _unstack_heads(o_ref, g, o, tq):
    for r in range(GROUP):
        h = GROUP * g + r
        o_ref[:, h * HEAD_DIM:(h + 1) * HEAD_DIM] = o[r * tq:(r + 1) * tq].astype(o_ref.dtype)


def _mem_attend(qm_ref, mk_ref, mv_ref, o_ref):
    for hm in range(N_MEM_HEADS):
        sl = slice(hm * HEAD_DIM, (hm + 1) * HEAD_DIM)
        s = _dot_nt(qm_ref[:, sl], mk_ref[:, sl]) * ATTN_SCALE
        m = jnp.max(s, axis=1, keepdims=True)
        p = jnp.exp(s - m)
        l = jnp.sum(p, axis=1, keepdims=True)
        om = _dot(p.astype(BF16), mv_ref[:, sl]) / l
        o_ref[:, Q_W + hm * HEAD_DIM:Q_W + (hm + 1) * HEAD_DIM] = om.astype(o_ref.dtype)


def _sort_key(x):
    bits = lax.bitcast_convert_type(x, jnp.int32)
    return bits ^ ((bits >> 31) & 0x7FFFFFFF)


def _dsa_kernel(q_ref, qm_ref, qi_ref, tail_ref, k_ref, vt_ref, kid_ref, mk_ref, mv_ref,
                o_ref, keys_ref, hi_ref, lo_ref, sa_ref, sb_ref, bias_ref, p_ref, acc_ref,
                *, tq, tk, s_valid, pos0, topk):
    i = pl.program_id(1)
    row0 = pos0 + i * tq
    qpos = row0 + lax.broadcasted_iota(jnp.int32, (1, tq), 1)
    lim = jnp.minimum(((qpos >> CHUNK_SHIFT) + 1) * CHUNK, s_valid)
    lim_max = jnp.minimum((((row0 + tq - 1) >> CHUNK_SHIFT) + 1) * CHUNK, s_valid)
    n_kt = (lim_max + tk - 1) // tk

    lane = lax.broadcasted_iota(jnp.int32, (tq, LANES), 1)
    pairs = []
    for p in range(IDX_HEADS // 2):
        qp = qi_ref[:, p * LANES:(p + 1) * LANES]
        zero = jnp.zeros_like(qp)
        pairs.append(jnp.concatenate(
            [jnp.where(lane < IDX_DIM, qp, zero), jnp.where(lane >= IDX_DIM, qp, zero)], axis=0))
    wt = tail_ref[...].T[IDX_DIM:IDX_DIM + IDX_HEADS, :] * IDX_W_SCALE
    krow = lax.broadcasted_iota(jnp.int32, (tk, tq), 0)

    def idx_body(kt, carry):
        kid = kid_ref[pl.ds(pl.multiple_of(kt * tk, tk), tk), :]
        acc = jnp.zeros((tk, tq), F32)
        for p in range(IDX_HEADS // 2):
            rel = jnp.maximum(_dot_nt(kid, pairs[p]), 0.0)
            acc = acc + wt[2 * p:2 * p + 1, :] * rel[:, :tq]
            acc = acc + wt[2 * p + 1:2 * p + 2, :] * rel[:, tq:]
        score = jnp.where(krow + kt * tk < lim, acc, NEG_INF)
        key = _sort_key(score)
        keys_ref[kt] = key
        hi_ref[kt] = (key >> 16).astype(jnp.int16)
        lo_ref[kt] = (key ^ 0x8000).astype(jnp.int16)
        return carry

    lax.fori_loop(0, n_kt, idx_body, 0)

    n_part = 4
    sub16 = 16

    def count16(ref, pred, cand):
        candb = jnp.broadcast_to(cand.astype(jnp.int16), (sub16, tq))

        def body(kt, cnts):
            xx = ref[kt]
            cnts = list(cnts)
            for r in range(tk // sub16):
                hit = pred(xx[r * sub16:(r + 1) * sub16, :], candb).astype(jnp.int16)
                cnts[r % n_part] = cnts[r % n_part] + hit
            return tuple(cnts)

        zero = jnp.zeros((sub16, tq), jnp.int16)
        cnts = lax.fori_loop(0, n_kt, body, (zero,) * n_part)
        total = sum(c.astype(F32) for c in cnts)
        return jnp.sum(total, axis=0, keepdims=True)

    ge = lambda a, b: a >= b
    gt = lambda a, b: a > b
    i16_min = -(2 ** 15)

    def search16(ref, need):
        zeros = jnp.zeros((1, tq), jnp.int32)
        thr0 = jnp.where(count16(ref, ge, zeros) >= need, zeros, jnp.full_like(zeros, i16_min))

        def step(it, thr):
            cand = thr | lax.shift_left(jnp.int32(1), 14 - it)
            return jnp.where(count16(ref, ge, cand) >= need, cand, thr)

        return lax.fori_loop(0, 15, step, thr0)

    need = jnp.full((1, tq), float(topk), F32)
    thr_hi = search16(hi_ref, need)
    need_lo = need - count16(hi_ref, gt, thr_hi)

    thr_hib = jnp.broadcast_to(thr_hi.astype(jnp.int16), (tk, tq))

    def keep_lo(kt, carry):
        lo_ref[kt] = jnp.where(hi_ref[kt] == thr_hib, lo_ref[kt], jnp.int16(i16_min))
        return carry

    lax.fori_loop(0, n_kt, keep_lo, 0)
    thr_lo = search16(lo_ref, need_lo)
    thr = (thr_hi << 16) | ((thr_lo ^ 0x8000) & 0xFFFF)
    thr_b = jnp.broadcast_to(thr, (tk, tq))

    def count32(pred):
        def body(kt, cnt):
            hit = jnp.where(pred(keys_ref[kt], kt * tk), 1, 0)
            for r in range(tk // 8):
                cnt = cnt + hit[r * 8:(r + 1) * 8, :]
            return cnt
        cnt = lax.fori_loop(0, n_kt, body, jnp.zeros((8, tq), jnp.int32))
        return jnp.sum(cnt.astype(F32), axis=0, keepdims=True)

    surplus = jnp.where(thr > NEG_INF_KEY, count32(lambda kk, k0: kk >= thr_b) - need, 0.0)

    @pl.when(jnp.max(surplus) > 0.0)
    def _():
        keep = count32(lambda kk, k0: kk == thr_b) - surplus
        idx_bits = (keys_ref.shape[0] * tk).bit_length()

        def step(it, bound):
            cand = bound | lax.shift_left(jnp.int32(1), idx_bits - 1 - it)
            cand_b = jnp.broadcast_to(cand, (tk, tq))
            below = count32(lambda kk, k0: jnp.logical_and(kk == thr_b, krow + k0 < cand_b))
            return jnp.where(below <= keep, cand, bound)

        bound = lax.fori_loop(0, idx_bits, step, jnp.zeros((1, tq), jnp.int32))
        bound_b = jnp.broadcast_to(bound, (tk, tq))

        def demote(kt, carry):
            kk = keys_ref[kt]
            drop = jnp.logical_and(kk == thr_b, krow + kt * tk >= bound_b)
            keys_ref[kt] = jnp.where(drop, thr_b - 1, kk)
            return carry

        lax.fori_loop(0, n_kt, demote, 0)

    thr = jnp.maximum(thr, NEG_INF_KEY + 1)

    half = tk // 2
    thr_h = jnp.broadcast_to(thr, (half, tq))
    heads = [(h // GROUP, h % GROUP) for h in range(N_HEADS)]

    def qk_stage(kt, sub, buf_ref):
        off = pl.multiple_of(kt * tk + sub * half, half)
        kk = keys_ref[kt, sub * half:(sub + 1) * half, :]
        bias_ref[...] = jnp.where(kk >= thr_h, 0.0, NEG_INF)
        for h, (g, r) in enumerate(heads):
            raw = _dot_nt(k_ref[pl.ds(off, half), g * HEAD_DIM:(g + 1) * HEAD_DIM],
                          q_ref[:, h * HEAD_DIM:(h + 1) * HEAD_DIM])
            buf_ref[g, :, r * tq:(r + 1) * tq] = raw + bias_ref[...]

    def softmax_stage(kt, sub, buf_ref, ms):
        out = []
        for h, (g, r) in enumerate(heads):
            m_new, a = [], []
            for c in range(r * tq, (r + 1) * tq, LANES):
                col = buf_ref[g, :, c:c + LANES]
                m_old = ms[g][:, c:c + LANES]
                m_c = jnp.maximum(m_old, jnp.max(col, axis=0, keepdims=True))
                p_ref[g, :, c:c + LANES] = jnp.exp2((col - m_c) * EXP2_SCALE).astype(BF16)
                m_new.append(m_c)
                a.append(jnp.exp2((m_old - m_c) * EXP2_SCALE))
            cols = slice(r * tq, (r + 1) * tq)
            acc_ref[g, :, cols] = (jnp.concatenate(a, axis=1) * acc_ref[g, :, cols]
                                   + _dot(vt_ref[2 * kt + sub, g], p_ref[g, :, cols]))
            out.append(jnp.concatenate(m_new, axis=1))
        return tuple(jnp.concatenate(out[GROUP * g:GROUP * (g + 1)], axis=1) for g in range(N_KV_HEADS))

    def att_body(kt, ms):
        qk_stage(kt, 1, sb_ref)
        ms = softmax_stage(kt, 0, sa_ref, ms)
        qk_stage(jnp.minimum(kt + 1, n_kt - 1), 0, sa_ref)
        return softmax_stage(kt, 1, sb_ref, ms)

    acc_ref[...] = jnp.zeros_like(acc_ref)
    qk_stage(0, 0, sa_ref)
    lax.fori_loop(0, n_kt, att_body,
                  tuple(jnp.full((1, GROUP * tq), NEG_INF, F32) for _ in range(N_KV_HEADS)))
    for h, (g, r) in enumerate(heads):
        acc = acc_ref[g, :, r * tq:(r + 1) * tq]
        o_t = acc[:HEAD_DIM] / acc[HEAD_DIM:HEAD_DIM + 1]
        o_ref[:, h * HEAD_DIM:(h + 1) * HEAD_DIM] = o_t.T.astype(o_ref.dtype)

    _mem_attend(qm_ref, mk_ref, mv_ref, o_ref)


def _dsa(q, qm, qi, tail, k, vt, kid, mk, mv, *, tq, s_valid, pos0, topk):
    n = q.shape[0]
    nb, n_half, _, vrows, half = vt.shape
    n_kt, tk = n_half // 2, 2 * half
    s_pad = k.shape[1]
    nq = n // nb // tq
    n_mem = mk.shape[0] // nb
    row = lambda b, i: (b * nq + i, 0)
    per_b = lambda b, i: (b, 0, 0)
    return pl.pallas_call(
        functools.partial(_dsa_kernel, tq=tq, tk=tk, s_valid=s_valid, pos0=pos0, topk=topk),
        grid=(nb, nq),
        in_specs=[
            pl.BlockSpec((tq, Q_W), row),
            pl.BlockSpec((tq, MQ_W), row),
            pl.BlockSpec((tq, IDXQ_W), row),
            pl.BlockSpec((tq, LANES), row),
            _resident((None, s_pad, KV_W), per_b),
            _resident((None, n_half, N_KV_HEADS, vrows, half), lambda b, i: (b, 0, 0, 0, 0)),
            _resident((None, s_pad, LANES), per_b),
            pl.BlockSpec((n_mem, MQ_W), lambda b, i: (b, 0)),
            pl.BlockSpec((n_mem, MQ_W), lambda b, i: (b, 0)),
        ],
        out_specs=pl.BlockSpec((tq, O_W), row),
        out_shape=jax.ShapeDtypeStruct((n, O_W), BF16),
        scratch_shapes=[pltpu.VMEM((n_kt, tk, tq), jnp.int32),
                        pltpu.VMEM((n_kt, tk, tq), jnp.int16),
                        pltpu.VMEM((n_kt, tk, tq), jnp.int16),
                        pltpu.VMEM((N_KV_HEADS, half, GROUP * tq), F32),
                        pltpu.VMEM((N_KV_HEADS, half, GROUP * tq), F32),
                        pltpu.VMEM((half, tq), F32),
                        pltpu.VMEM((N_KV_HEADS, half, GROUP * tq), BF16),
                        pltpu.VMEM((N_KV_HEADS, vrows, GROUP * tq), F32)],
        compiler_params=_params(2),
        name="dsa_attend",
    )(q, qm, qi, tail, k, vt, kid, mk, mv)


def _key_tiles_t(v, tk):
    nb, s, _ = v.shape
    vt = v.reshape(nb, s // tk, tk, N_KV_HEADS, HEAD_DIM).transpose(0, 1, 3, 4, 2)
    ones = jnp.ones((nb, s // tk, N_KV_HEADS, V_ONES, tk), v.dtype)
    return jnp.concatenate([vt, ones], axis=3)


def _band_table_kernel(rb_ref, o_ref, *, qc, band, prefix):
    h = pl.program_id(0)
    shape = o_ref.shape
    r = lax.broadcasted_iota(jnp.int32, shape, 0)
    w = lax.broadcasted_iota(jnp.int32, shape, 1)
    idx = jnp.clip(r - w + prefix, -REL_CLIP, REL_CLIP) + REL_CLIP
    lo = (r // qc) * qc
    inband = jnp.logical_and(w >= lo, w < lo + band)

    def body(t, acc):
        return jnp.where(idx == t, rb_ref[h, t], acc)

    acc = lax.fori_loop(0, 2 * REL_CLIP + 1, body, jnp.zeros(shape, F32))
    o_ref[...] = jnp.where(inband, acc, NEG_INF)


def _band_table(rel_bias, rows, width, qc, band, prefix):
    nh = rel_bias.shape[0]
    return pl.pallas_call(
        functools.partial(_band_table_kernel, qc=qc, band=band, prefix=prefix),
        grid=(nh,),
        in_specs=[pl.BlockSpec(memory_space=pltpu.SMEM)],
        out_specs=pl.BlockSpec((None, rows, width), lambda h: (h, 0, 0)),
        out_shape=jax.ShapeDtypeStruct((nh, rows, width), F32),
        compiler_params=_params(1),
        name="band_table",
    )(rel_bias)


def _band_kernel(q_ref, qm_ref, k_ref, v_ref, tab_ref, mk_ref, mv_ref, o_ref, *, tq, width, kp0):
    i = pl.program_id(1)
    off = pl.multiple_of(i * tq, tq)
    kp_ok = (kp0 + i * tq + lax.broadcasted_iota(jnp.int32, (GROUP * tq, width), 1)) >= 0
    for g in range(N_KV_HEADS):
        hs = slice(g * HEAD_DIM, (g + 1) * HEAD_DIM)
        q3 = _stack_heads(q_ref, g)
        s = _dot_nt(q3, k_ref[pl.ds(off, width), hs]) * ATTN_SCALE
        s = s + jnp.concatenate([tab_ref[GROUP * g + r] for r in range(GROUP)], axis=0)
        s = jnp.where(kp_ok, s, NEG_INF)
        m = jnp.max(s, axis=1, keepdims=True)
        p = jnp.exp(s - m)
        l = jnp.sum(p, axis=1, keepdims=True)
        o = _dot(p.astype(BF16), v_ref[pl.ds(off, width), hs]) / l
        _unstack_heads(o_ref, g, o, tq)
    _mem_attend(qm_ref, mk_ref, mv_ref, o_ref)


def _band(q, qm, k_all, v_all, table, mk, mv, *, tq, kp0):
    n = q.shape[0]
    nb, l_pad, _ = k_all.shape
    nq = n // nb // tq
    n_mem = mk.shape[0] // nb
    width = table.shape[2]
    row = lambda b, i: (b * nq + i, 0)
    per_b = lambda b, i: (b, 0, 0)
    return pl.pallas_call(
        functools.partial(_band_kernel, tq=tq, width=width, kp0=kp0),
        grid=(nb, nq),
        in_specs=[
            pl.BlockSpec((tq, Q_W), row),
            pl.BlockSpec((tq, MQ_W), row),
            _resident((None, l_pad, KV_W), per_b),
            _resident((None, l_pad, KV_W), per_b),
            _resident(table.shape, lambda b, i: (0, 0, 0)),
            pl.BlockSpec((n_mem, MQ_W), lambda b, i: (b, 0)),
            pl.BlockSpec((n_mem, MQ_W), lambda b, i: (b, 0)),
        ],
        out_specs=pl.BlockSpec((tq, O_W), row),
        out_shape=jax.ShapeDtypeStruct((n, O_W), BF16),
        compiler_params=_params(2),
        name="band_attend",
    )(q, qm, k_all, v_all, table, mk, mv)


def _pad_rows(a, rows):
    return jnp.pad(a, ((0, 0), (0, rows - a.shape[1]), (0, 0)))


def kernel(x_prompt, x_sample, mem_prompt, cache_a_k, cache_a_v, cache_a_kidx, cache_b_k, cache_b_v,
           cache_mem_k, cache_mem_v, w_in_a, w_in_b, w_o, rel_bias, w_mem_kv,
           w_ffn1_in, w_ffn1_out, w_ffn2_in, w_ffn2_out, ln_g, ln_b):
    depth = w_o.shape[0]
    alpha = (2 * depth) ** 0.25
    bp, tp, d = x_prompt.shape
    bs, ts, _ = x_sample.shape
    past = cache_a_k.shape[2]
    n_mem = mem_prompt.shape[1]
    assert tp % DSA_TK == 0 and ts % 16 == 0 and ts <= CHUNK
    assert cache_b_k.shape[2] == WIN_B and tp >= WIN_B

    pa_pad = -(-PROJ_A // LANES) * LANES
    wa = jnp.pad(w_in_a, ((0, 0), (0, 0), (0, pa_pad - PROJ_A))).astype(BF16)
    wb, wo = w_in_b.astype(BF16), w_o.astype(BF16)
    w1i, w1o = w_ffn1_in.astype(BF16), (0.5 * w_ffn1_out).astype(BF16)
    w2i, w2o = w_ffn2_in.astype(BF16), (0.5 * w_ffn2_out).astype(BF16)

    memf, memb = _mem_kv(mem_prompt.reshape(bp * n_mem, d), w_mem_kv.astype(BF16))
    mk_p = memf[..., :MQ_W].reshape(depth, bp, n_mem, N_MEM_HEADS, HEAD_DIM)
    mv_p = memf[..., MQ_W:].reshape(depth, bp, n_mem, N_MEM_HEADS, HEAD_DIM)
    mkb_p, mvb_p = memb[..., :MQ_W], memb[..., MQ_W:]
    mkb_s = cache_mem_k.astype(BF16).reshape(depth, bs * n_mem, MQ_W)
    mvb_s = cache_mem_v.astype(BF16).reshape(depth, bs * n_mem, MQ_W)

    tab_p = _rope_tables(jnp.arange(tp))
    tab_s = jnp.tile(_rope_tables(past + jnp.arange(ts)), (1, bs, 1))

    xp = x_prompt.reshape(bp * tp, d)
    xs = x_sample.reshape(bs * ts, d)
    st = {k: [] for k in ("akp", "avp", "aip", "bkp", "bvp", "aks", "avs", "ais", "bks", "bvs")}
    heads = lambda a, b, t: a.reshape(b, t, N_KV_HEADS, HEAD_DIM)

    for i in range(depth):
        j = i // 2
        g, b = ln_g[i], ln_b[i]
        xp = _ffn_ln(xp, w1i, w1o, i, g[0], b[0], alpha)
        xs = _ffn_ln(xs, w1i, w1o, i, g[0], b[0], alpha)
        if i % 2 == 0:
            q, kf, vf, kb, vb, qm, qi, tail, kid, vt = _proj(xp, wa, j, tab_p, True)
            ap = _dsa(q, qm, qi, tail, kb.reshape(bp, tp, KV_W),
                      vt.reshape((bp, tp // DSA_HALF) + vt.shape[1:]),
                      kid.reshape(bp, tp, LANES), mkb_p[i], mvb_p[i],
                      tq=min(TQ, tp), s_valid=tp, pos0=0, topk=min(TOPK_MAX, tp // 4))
            st["akp"].append(heads(kf, bp, tp))
            st["avp"].append(heads(vf, bp, tp))
            st["aip"].append(tail[:, :IDX_DIM].reshape(bp, tp, IDX_DIM))

            q, kf, vf, kb, vb, qm, qi, tail, kid = _proj(xs, wa, j, tab_s, True)
            tk = DSA_TK
            s_valid = past + ts
            s_pad = -(-s_valid // tk) * tk
            ci = cache_a_kidx[j].astype(BF16)
            k_all = _pad_rows(jnp.concatenate(
                [cache_a_k[j].astype(BF16).reshape(bs, past, KV_W), kb.reshape(bs, ts, KV_W)], 1), s_pad)
            v_all = _pad_rows(jnp.concatenate(
                [cache_a_v[j].astype(BF16).reshape(bs, past, KV_W), vb.reshape(bs, ts, KV_W)], 1), s_pad)
            kid_all = _pad_rows(jnp.concatenate(
                [jnp.concatenate([ci, ci], -1), kid.reshape(bs, ts, LANES)], 1), s_pad)
            tq = TQ_SAMPLE
            qpad = lambda a: _pad_rows(a.reshape(bs, ts, a.shape[-1]), tq).reshape(bs * tq, a.shape[-1])
            as_ = _dsa(qpad(q), qpad(qm), qpad(qi), qpad(tail), k_all, _key_tiles_t(v_all, tk // 2), kid_all,
                       mkb_s[i], mvb_s[i], tq=tq, s_valid=s_valid, pos0=past,
                       topk=min(TOPK_MAX, s_valid // 4))
            as_ = as_.reshape(bs, tq, O_W)[:, :ts].reshape(bs * ts, O_W)
            st["aks"].append(heads(kf, bs, ts))
            st["avs"].append(heads(vf, bs, ts))
            st["ais"].append(tail[:, :IDX_DIM].reshape(bs, ts, IDX_DIM))
        else:
            q, kf, vf, kb, vb, qm = _proj(xp, wb, j, None, False)
            tq = LANES
            table = _band_table(rel_bias[j], tq, tq + WIN_B, CHUNK, WIN_B + CHUNK, WIN_B)
            zpad = jnp.zeros((bp, WIN_B, KV_W), BF16)
            ap = _band(q, qm, jnp.concatenate([zpad, kb.reshape(bp, tp, KV_W)], 1),
                       jnp.concatenate([zpad, vb.reshape(bp, tp, KV_W)], 1),
                       table, mkb_p[i], mvb_p[i], tq=tq, kp0=-WIN_B)
            n_keep = min(WIN_B, tp)
            st["bkp"].append(heads(kf, bp, tp)[:, tp - n_keep:])
            st["bvp"].append(heads(vf, bp, tp)[:, tp - n_keep:])

            q, kf, vf, kb, vb, qm = _proj(xs, wb, j, None, False)
            width = -(-(WIN_B + ts) // LANES) * LANES
            table = _band_table(rel_bias[j], ts, width, ts, WIN_B + ts, WIN_B)
            k_all = _pad_rows(jnp.concatenate(
                [cache_b_k[j].astype(BF16).reshape(bs, WIN_B, KV_W), kb.reshape(bs, ts, KV_W)], 1), width)
            v_all = _pad_rows(jnp.concatenate(
                [cache_b_v[j].astype(BF16).reshape(bs, WIN_B, KV_W), vb.reshape(bs, ts, KV_W)], 1), width)
            as_ = _band(q, qm, k_all, v_all, table, mkb_s[i], mvb_s[i], tq=ts, kp0=past - WIN_B)
            st["bks"].append(jnp.concatenate([cache_b_k[j][:, ts:], heads(kf, bs, ts)], 1))
            st["bvs"].append(jnp.concatenate([cache_b_v[j][:, ts:], heads(vf, bs, ts)], 1))

        xp2 = _oproj_ln(ap, wo, i, xp, g[1], b[1], alpha)
        xs2 = _oproj_ln(as_, wo, i, xs, g[1], b[1], alpha)
        xp = _ffn_ln(xp2, w2i, w2o, i, g[2], b[2], alpha)
        xs = _ffn_ln(xs2, w2i, w2o, i, g[2], b[2], alpha)

    stack = lambda name: jnp.stack(st[name])
    return (xp.reshape(bp, tp, d), xs.reshape(bs, ts, d),
            stack("akp"), stack("avp"), stack("aip"), stack("bkp"), stack("bvp"), mk_p, mv_p,
            stack("aks"), stack("avs"), stack("ais"), stack("bks"), stack("bvs"))
```

```python
import functools

import numpy as np
import jax
import jax.numpy as jnp
from jax import lax
from jax.experimental import pallas as pl
from jax.experimental.pallas import tpu as pltpu

CHUNK = 64
CHUNK_SHIFT = 6
N_HEADS = 12
N_KV_HEADS = 4
HEAD_DIM = 128
GROUP = N_HEADS // N_KV_HEADS
N_MEM_HEADS = 4
IDX_HEADS = 8
IDX_DIM = 64
TOPK_MAX = 256
WIN_B = 8 * CHUNK
REL_CLIP = 128
ROPE_THETA = 10000.0
LN_EPS = 1e-5
Q_W = N_HEADS * HEAD_DIM
KV_W = N_KV_HEADS * HEAD_DIM
MQ_W = N_MEM_HEADS * HEAD_DIM
IDXQ_W = IDX_HEADS * IDX_DIM
O_W = Q_W + MQ_W
PROJ_A = Q_W + 2 * KV_W + MQ_W + IDXQ_W + IDX_DIM + IDX_HEADS
PROJ_B = Q_W + 2 * KV_W + MQ_W
ATTN_SCALE = HEAD_DIM ** -0.5
IDX_W_SCALE = (IDX_HEADS * IDX_DIM) ** -0.5
NEG_INF = -1e30
EXP2_SCALE = ATTN_SCALE * float(np.log2(np.e))

LANES = 128
FFN_TM = 1024
FFN_TF = 512
TQ = 2 * LANES
TQ_SAMPLE = LANES
DSA_TK = 4 * LANES
DSA_HALF = DSA_TK // 2
V_ONES = 16
MAX_PEEL_PASSES = 4.0
VMEM_LIMIT = 62 * 1024 * 1024

F32 = jnp.float32
BF16 = jnp.bfloat16
INT_MIN = -(2 ** 31)
NEG_INF_KEY = int(np.array(NEG_INF, np.float32).view(np.int32)) ^ 0x7FFFFFFF


def _dot(a, b):
    return jnp.dot(a, b, preferred_element_type=F32)


def _dot_nt(a, b):
    return lax.dot_general(a, b, (((1,), (1,)), ((), ())), preferred_element_type=F32)


def _params(n_axes):
    return pltpu.CompilerParams(dimension_semantics=("arbitrary",) * n_axes,
                                vmem_limit_bytes=VMEM_LIMIT)


def _resident(block_shape, index_map):
    return pl.BlockSpec(block_shape, index_map, pipeline_mode=pl.Buffered(1))


def _layer_norm(y, g, b):
    mu = jnp.mean(y, axis=-1, keepdims=True)
    d = y - mu
    var = jnp.mean(d * d, axis=-1, keepdims=True)
    return d * lax.rsqrt(var + LN_EPS) * g + b


def _ffn_ln_kernel(x_ref, wg_ref, wu_ref, wo_ref, g_ref, b_ref, o_ref, *, alpha):
    j = pl.program_id(1)

    @pl.when(j == 0)
    def _():
        o_ref[...] = alpha * x_ref[...]

    xb = x_ref[...].astype(BF16)
    gate = _dot(xb, wg_ref[...])
    up = _dot(xb, wu_ref[...])
    h = (gate / (1.0 + jnp.exp(-gate))) * up
    o_ref[...] += _dot(h.astype(BF16), wo_ref[...])

    @pl.when(j == pl.num_programs(1) - 1)
    def _():
        o_ref[...] = _layer_norm(o_ref[...], g_ref[...], b_ref[...])


def _ffn_ln(x, w_in, w_out, layer, g, b, alpha):
    n, d = x.shape
    dff = w_out.shape[1]
    tm = min(FFN_TM, n)
    tf = FFN_TF if dff % FFN_TF == 0 else LANES
    nj = dff // tf
    return pl.pallas_call(
        functools.partial(_ffn_ln_kernel, alpha=alpha),
        grid=(n // tm, nj),
        in_specs=[
            pl.BlockSpec((tm, d), lambda i, j: (i, 0)),
            pl.BlockSpec((None, d, tf), lambda i, j: (layer, 0, j)),
            pl.BlockSpec((None, d, tf), lambda i, j: (layer, 0, j + nj)),
            pl.BlockSpec((None, tf, d), lambda i, j: (layer, j, 0)),
            pl.BlockSpec((1, d), lambda i, j: (0, 0)),
            pl.BlockSpec((1, d), lambda i, j: (0, 0)),
        ],
        out_specs=pl.BlockSpec((tm, d), lambda i, j: (i, 0)),
        out_shape=jax.ShapeDtypeStruct((n, d), F32),
        compiler_params=_params(2),
        name="ffn_ln",
    )(x, w_in, w_in, w_out, g.reshape(1, d), b.reshape(1, d))


def _oproj_ln_kernel(a_ref, w_ref, x_ref, g_ref, b_ref, o_ref, *, alpha):
    y = alpha * x_ref[...] + _dot(a_ref[...], w_ref[...])
    o_ref[...] = _layer_norm(y, g_ref[...], b_ref[...])


def _oproj_ln(a, w, layer, x, g, b, alpha):
    n, d = x.shape
    ow = a.shape[1]
    tm = min(512, n)
    return pl.pallas_call(
        functools.partial(_oproj_ln_kernel, alpha=alpha),
        grid=(n // tm,),
        in_specs=[
            pl.BlockSpec((tm, ow), lambda i: (i, 0)),
            _resident((None, ow, d), lambda i: (layer, 0, 0)),
            pl.BlockSpec((tm, d), lambda i: (i, 0)),
            pl.BlockSpec((1, d), lambda i: (0, 0)),
            pl.BlockSpec((1, d), lambda i: (0, 0)),
        ],
        out_specs=pl.BlockSpec((tm, d), lambda i: (i, 0)),
        out_shape=jax.ShapeDtypeStruct((n, d), F32),
        compiler_params=_params(1),
        name="oproj_ln",
    )(a, w, x, g.reshape(1, d), b.reshape(1, d))


def _mem_kv_kernel(x_ref, w_ref, of_ref, ob_ref):
    y = _dot(x_ref[...].astype(BF16), w_ref[...])
    of_ref[...] = y
    ob_ref[...] = y.astype(BF16)


def _mem_kv(mem, w):
    n, d = mem.shape
    depth, _, c = w.shape
    return pl.pallas_call(
        _mem_kv_kernel,
        grid=(depth,),
        in_specs=[
            _resident((n, d), lambda l: (0, 0)),
            pl.BlockSpec((None, d, c), lambda l: (l, 0, 0)),
        ],
        out_specs=[
            pl.BlockSpec((None, n, c), lambda l: (l, 0, 0)),
            pl.BlockSpec((None, n, c), lambda l: (l, 0, 0)),
        ],
        out_shape=[jax.ShapeDtypeStruct((depth, n, c), F32),
                   jax.ShapeDtypeStruct((depth, n, c), BF16)],
        compiler_params=_params(1),
        name="mem_kv",
    )(mem, w)


def _rope_tables(pos):
    posf = pos.astype(F32)[:, None]

    def cs(half):
        inv_freq = ROPE_THETA ** (-jnp.arange(half, dtype=F32) / half)
        ang = posf * inv_freq[None, :]
        return jnp.cos(ang), jnp.sin(ang)

    c64, s64 = cs(HEAD_DIM // 2)
    c32, s32 = cs(IDX_DIM // 2)
    z32 = jnp.zeros_like(s32)
    cos128 = jnp.concatenate([c64, c64], -1)
    sin128 = jnp.concatenate([-s64, s64], -1)
    cos64 = jnp.concatenate([c32] * 4, -1)
    sa64 = jnp.concatenate([-s32, z32, -s32, z32], -1)
    sb64 = jnp.concatenate([z32, s32, z32, s32], -1)
    lane = jnp.arange(LANES)[None, :]
    first = lane < IDX_DIM
    return jnp.stack([cos128, sin128, cos64, sa64, sb64,
                      jnp.where(first, cos64, 1.0), jnp.where(first, sa64, 0.0),
                      jnp.where(first, sb64, 0.0)])


def _proj_kernel(x_ref, w_ref, *refs, layer_a, emit_vt):
    xb = x_ref[...].astype(BF16)
    vt_ref = None
    if layer_a:
        tab_ref, q_ref, kf_ref, vf_ref, kb_ref, vb_ref, qm_ref, qi_ref, tail_ref, kid_ref = refs[:10]
        if emit_vt:
            vt_ref = refs[10]
    else:
        q_ref, kf_ref, vf_ref, kb_ref, vb_ref, qm_ref = refs

    def rope128(y):
        return y * tab_ref[0] + pltpu.roll(y, HEAD_DIM // 2, axis=1) * tab_ref[1]

    def rope64(y, t0):
        return (y * tab_ref[t0] + pltpu.roll(y, LANES - IDX_DIM // 2, axis=1) * tab_ref[t0 + 1]
                + pltpu.roll(y, IDX_DIM // 2, axis=1) * tab_ref[t0 + 2])

    def section(col0, width, fn):
        step = 512
        for c0 in range(0, width, step):
            cw = min(step, width - c0)
            y = _dot(xb, w_ref[:, col0 + c0:col0 + c0 + cw])
            for s in range(cw // LANES):
                fn(c0 + s * LANES, y[:, s * LANES:(s + 1) * LANES])

    def put(ref, fn=None):
        def f(off, y):
            ref[:, off:off + LANES] = (y if fn is None else fn(y)).astype(ref.dtype)
        return f

    def put2(ref_f, ref_b, fn=None):
        def f(off, y):
            y = y if fn is None else fn(y)
            ref_f[:, off // HEAD_DIM, :] = y
            ref_b[:, off:off + LANES] = y.astype(BF16)
        return f

    put_v = put2(vf_ref, vb_ref)

    def put_v_and_vt(off, y):
        put_v(off, y)
        g = off // HEAD_DIM
        for u in range(y.shape[0] // DSA_HALF):
            vt_ref[u, g, :HEAD_DIM, :] = y[u * DSA_HALF:(u + 1) * DSA_HALF, :].T.astype(BF16)
            vt_ref[u, g, HEAD_DIM:, :] = jnp.ones((V_ONES, DSA_HALF), BF16)

    rq = rope128 if layer_a else None
    section(0, Q_W, put(q_ref, rq))
    section(Q_W, KV_W, put2(kf_ref, kb_ref, rq))
    section(Q_W + KV_W, KV_W, put_v_and_vt if emit_vt else put_v)
    section(Q_W + 2 * KV_W, MQ_W, put(qm_ref))
    if layer_a:
        c = Q_W + 2 * KV_W + MQ_W
        section(c, IDXQ_W, put(qi_ref, lambda y: rope64(y, 2)))

        def tail(off, y):
            t = rope64(y, 5)
            tail_ref[...] = t
            lane = lax.broadcasted_iota(jnp.int32, t.shape, 1)
            dup = jnp.where(lane < IDX_DIM, t, pltpu.roll(t, IDX_DIM, axis=1))
            kid_ref[...] = dup.astype(BF16)
        section(c + IDXQ_W, LANES, tail)


def _proj(x, w, layer, tab, layer_a):
    n, d = x.shape
    pw = w.shape[2]
    tm = min(512, n)
    row = lambda i: (i, 0)
    in_specs = [pl.BlockSpec((tm, d), row), _resident((None, d, pw), lambda i: (layer, 0, 0))]
    args = [x, w]
    kv_state = (N_KV_HEADS, HEAD_DIM)
    outs = [((Q_W,), BF16), (kv_state, F32), (kv_state, F32), ((KV_W,), BF16), ((KV_W,), BF16),
            ((MQ_W,), BF16)]
    if layer_a:
        nt = tab.shape[1] // tm
        in_specs.append(pl.BlockSpec((8, tm, LANES), lambda i: (0, i % nt, 0)))
        args.append(tab)
        outs += [((IDXQ_W,), BF16), ((LANES,), F32), ((LANES,), BF16)]
    emit_vt = layer_a and tm % DSA_HALF == 0
    out_specs = [pl.BlockSpec((tm,) + wd, lambda i, nd=len(wd): (i,) + (0,) * nd) for wd, _ in outs]
    out_shape = [jax.ShapeDtypeStruct((n,) + wd, dt) for wd, dt in outs]
    if emit_vt:
        vt_tile = (N_KV_HEADS, HEAD_DIM + V_ONES, DSA_HALF)
        out_specs.append(pl.BlockSpec((tm // DSA_HALF,) + vt_tile, lambda i: (i, 0, 0, 0)))
        out_shape.append(jax.ShapeDtypeStruct((n // DSA_HALF,) + vt_tile, BF16))
    return pl.pallas_call(
        functools.partial(_proj_kernel, layer_a=layer_a, emit_vt=emit_vt),
        grid=(n // tm,),
        in_specs=in_specs,
        out_specs=out_specs,
        out_shape=out_shape,
        compiler_params=_params(1),
        name="proj_a" if layer_a else "proj_b",
    )(*args)


def _stack_heads(q_ref, g):
    return jnp.concatenate(
        [q_ref[:, (GROUP * g + r) * HEAD_DIM:(GROUP * g + r + 1) * HEAD_DIM] for r in range(GROUP)],
        axis=0)


def _unstack_heads(o_ref, g, o, tq):
    for r in range(GROUP):
        h = GROUP * g + r
        o_ref[:, h * HEAD_DIM:(h + 1) * HEAD_DIM] = o[r * tq:(r + 1) * tq].astype(o_ref.dtype)


def _mem_attend(qm_ref, mk_ref, mv_ref, o_ref):
    for hm in range(N_MEM_HEADS):
        sl = slice(hm * HEAD_DIM, (hm + 1) * HEAD_DIM)
        s = _dot_nt(qm_ref[:, sl], mk_ref[:, sl]) * ATTN_SCALE
        m = jnp.max(s, axis=1, keepdims=True)
        p = jnp.exp(s - m)
        l = jnp.sum(p, axis=1, keepdims=True)
        om = _dot(p.astype(BF16), mv_ref[:, sl]) / l
        o_ref[:, Q_W + hm * HEAD_DIM:Q_W + (hm + 1) * HEAD_DIM] = om.astype(o_ref.dtype)


def _sort_key(x):
    bits = lax.bitcast_convert_type(x, jnp.int32)
    return bits ^ ((bits >> 31) & 0x7FFFFFFF)


def _dsa_kernel(q_ref, qm_ref, qi_ref, tail_ref, k_ref, vt_ref, kid_ref, mk_ref, mv_ref,
                o_ref, keys_ref, hi_ref, lo_ref, sa_ref, sb_ref, *, tq, tk, s_valid, pos0, topk):
    i = pl.program_id(1)
    row0 = pos0 + i * tq
    qpos = row0 + lax.broadcasted_iota(jnp.int32, (1, tq), 1)
    lim = jnp.minimum(((qpos >> CHUNK_SHIFT) + 1) * CHUNK, s_valid)
    lim_max = jnp.minimum((((row0 + tq - 1) >> CHUNK_SHIFT) + 1) * CHUNK, s_valid)
    n_kt = (lim_max + tk - 1) // tk

    lane = lax.broadcasted_iota(jnp.int32, (tq, LANES), 1)
    pairs = []
    for p in range(IDX_HEADS // 2):
        qp = qi_ref[:, p * LANES:(p + 1) * LANES]
        zero = jnp.zeros_like(qp)
        pairs.append(jnp.concatenate(
            [jnp.where(lane < IDX_DIM, qp, zero), jnp.where(lane >= IDX_DIM, qp, zero)], axis=0))
    wt = tail_ref[...].T[IDX_DIM:IDX_DIM + IDX_HEADS, :] * IDX_W_SCALE
    krow = lax.broadcasted_iota(jnp.int32, (tk, tq), 0)

    def idx_body(kt, carry):
        kid = kid_ref[pl.ds(pl.multiple_of(kt * tk, tk), tk), :]
        acc = jnp.zeros((tk, tq), F32)
        for p in range(IDX_HEADS // 2):
            rel = jnp.maximum(_dot_nt(kid, pairs[p]), 0.0)
            acc = acc + wt[2 * p:2 * p + 1, :] * rel[:, :tq]
            acc = acc + wt[2 * p + 1:2 * p + 2, :] * rel[:, tq:]
        score = jnp.where(krow + kt * tk < lim, acc, NEG_INF)
        key = _sort_key(score)
        keys_ref[kt] = key
        hi_ref[kt] = (key >> 16).astype(jnp.int16)
        lo_ref[kt] = (key ^ 0x8000).astype(jnp.int16)
        return carry

    lax.fori_loop(0, n_kt, idx_body, 0)

    n_part = 4
    sub16 = 16

    def count16(ref, pred, cand):
        candb = jnp.broadcast_to(cand.astype(jnp.int16), (sub16, tq))

        def body(kt, cnts):
            xx = ref[kt]
            cnts = list(cnts)
            for r in range(tk // sub16):
                hit = pred(xx[r * sub16:(r + 1) * sub16, :], candb).astype(jnp.int16)
                cnts[r % n_part] = cnts[r % n_part] + hit
            return tuple(cnts)

        zero = jnp.zeros((sub16, tq), jnp.int16)
        cnts = lax.fori_loop(0, n_kt, body, (zero,) * n_part)
        total = sum(c.astype(F32) for c in cnts)
        return jnp.sum(total, axis=0, keepdims=True)

    ge = lambda a, b: a >= b
    gt = lambda a, b: a > b
    i16_min = -(2 ** 15)

    def search16(ref, need):
        zeros = jnp.zeros((1, tq), jnp.int32)
        thr0 = jnp.where(count16(ref, ge, zeros) >= need, zeros, jnp.full_like(zeros, i16_min))

        def step(it, thr):
            cand = thr | lax.shift_left(jnp.int32(1), 14 - it)
            return jnp.where(count16(ref, ge, cand) >= need, cand, thr)

        return lax.fori_loop(0, 15, step, thr0)

    need = jnp.full((1, tq), float(topk), F32)
    thr_hi = search16(hi_ref, need)
    need_lo = need - count16(hi_ref, gt, thr_hi)

    thr_hib = jnp.broadcast_to(thr_hi.astype(jnp.int16), (tk, tq))

    def keep_lo(kt, carry):
        lo_ref[kt] = jnp.where(hi_ref[kt] == thr_hib, lo_ref[kt], jnp.int16(i16_min))
        return carry

    lax.fori_loop(0, n_kt, keep_lo, 0)
    thr_lo = search16(lo_ref, need_lo)
    thr = (thr_hi << 16) | ((thr_lo ^ 0x8000) & 0xFFFF)
    thr_b = jnp.broadcast_to(thr, (tk, tq))

    def count32(pred):
        def body(kt, cnt):
            hit = jnp.where(pred(keys_ref[kt], kt * tk), 1, 0)
            for r in range(tk // 8):
                cnt = cnt + hit[r * 8:(r + 1) * 8, :]
            return cnt
        cnt = lax.fori_loop(0, n_kt, body, jnp.zeros((8, tq), jnp.int32))
        return jnp.sum(cnt.astype(F32), axis=0, keepdims=True)

    surplus = jnp.where(thr > NEG_INF_KEY, count32(lambda kk, k0: kk >= thr_b) - need, 0.0)

    max_surplus = jnp.max(surplus)

    @pl.when(max_surplus > 0.0)
    def _():
        idx_bits = (keys_ref.shape[0] * tk).bit_length()

        def by_peeling():
            def peel(j, bound):
                bound_b = jnp.broadcast_to(bound, (tk, tq))

                def body(kt, best):
                    idx = krow + kt * tk
                    tie = jnp.logical_and(keys_ref[kt] == thr_b, idx < bound_b)
                    cand = jnp.where(tie, idx, -1)
                    for r in range(tk // 8):
                        best = jnp.maximum(best, cand[r * 8:(r + 1) * 8, :])
                    return best

                best = lax.fori_loop(0, n_kt, body, jnp.full((8, tq), -1, jnp.int32))
                top = jnp.max(best.astype(F32), axis=0, keepdims=True).astype(jnp.int32)
                return jnp.where(surplus > j.astype(F32), top, bound)

            return lax.fori_loop(0, max_surplus.astype(jnp.int32), peel,
                                 jnp.full((1, tq), 2 ** idx_bits, jnp.int32))

        def by_bisection():
            keep = count32(lambda kk, k0: kk == thr_b) - surplus

            def step(it, bound):
                cand = bound | lax.shift_left(jnp.int32(1), idx_bits - 1 - it)
                cand_b = jnp.broadcast_to(cand, (tk, tq))
                below = count32(lambda kk, k0: jnp.logical_and(kk == thr_b, krow + k0 < cand_b))
                return jnp.where(below <= keep, cand, bound)

            return lax.fori_loop(0, idx_bits, step, jnp.zeros((1, tq), jnp.int32))

        bound = lax.cond(max_surplus <= MAX_PEEL_PASSES, by_peeling, by_bisection)
        bound_b = jnp.broadcast_to(bound, (tk, tq))

        def demote(kt, carry):
            kk = keys_ref[kt]
            drop = jnp.logical_and(kk == thr_b, krow + kt * tk >= bound_b)
            keys_ref[kt] = jnp.where(drop, thr_b - 1, kk)
            return carry

        lax.fori_loop(0, n_kt, demote, 0)

    thr = jnp.maximum(thr, NEG_INF_KEY + 1)

    q3 = [_stack_heads(q_ref, g) for g in range(N_KV_HEADS)]

    half = tk // 2
    thr_h = jnp.broadcast_to(thr, (half, tq))

    def qk_stage(kt, sub, buf_ref):
        off = pl.multiple_of(kt * tk + sub * half, half)
        kk = keys_ref[kt, sub * half:(sub + 1) * half, :]
        bias = jnp.where(kk >= thr_h, 0.0, NEG_INF)
        bias = jnp.concatenate([bias] * GROUP, axis=1)
        for g in range(N_KV_HEADS):
            hs = slice(g * HEAD_DIM, (g + 1) * HEAD_DIM)
            buf_ref[g] = _dot_nt(k_ref[pl.ds(off, half), hs], q3[g]) + bias

    def softmax_stage(kt, sub, buf_ref, carry):
        out = []
        for g in range(N_KV_HEADS):
            m, acc = carry[g]
            m_new = jnp.maximum(m, jnp.max(buf_ref[g], axis=0, keepdims=True))
            a = jnp.exp2((m - m_new) * EXP2_SCALE)
            p = jnp.exp2((buf_ref[g] - m_new) * EXP2_SCALE)
            acc = a * acc + _dot(vt_ref[2 * kt + sub, g], p.astype(BF16))
            out.append((m_new, acc))
        return tuple(out)

    def att_body(kt, carry):
        qk_stage(kt, 1, sb_ref)
        carry = softmax_stage(kt, 0, sa_ref, carry)
        qk_stage(jnp.minimum(kt + 1, n_kt - 1), 0, sa_ref)
        return softmax_stage(kt, 1, sb_ref, carry)

    init = tuple((jnp.full((1, GROUP * tq), NEG_INF, F32),
                  jnp.zeros((HEAD_DIM + V_ONES, GROUP * tq), F32)) for _ in range(N_KV_HEADS))
    qk_stage(0, 0, sa_ref)
    res = lax.fori_loop(0, n_kt, att_body, init)
    for g in range(N_KV_HEADS):
        acc = res[g][1]
        o_t = acc[:HEAD_DIM] / acc[HEAD_DIM:HEAD_DIM + 1]
        for r in range(GROUP):
            h = GROUP * g + r
            o_ref[:, h * HEAD_DIM:(h + 1) * HEAD_DIM] = o_t[:, r * tq:(r + 1) * tq].T.astype(o_ref.dtype)

    _mem_attend(qm_ref, mk_ref, mv_ref, o_ref)


def _dsa(q, qm, qi, tail, k, vt, kid, mk, mv, *, tq, s_valid, pos0, topk):
    n = q.shape[0]
    nb, n_half, _, vrows, half = vt.shape
    n_kt, tk = n_half // 2, 2 * half
    s_pad = k.shape[1]
    nq = n // nb // tq
    n_mem = mk.shape[0] // nb
    row = lambda b, i: (b * nq + i, 0)
    per_b = lambda b, i: (b, 0, 0)
    return pl.pallas_call(
        functools.partial(_dsa_kernel, tq=tq, tk=tk, s_valid=s_valid, pos0=pos0, topk=topk),
        grid=(nb, nq),
        in_specs=[
            pl.BlockSpec((tq, Q_W), row),
            pl.BlockSpec((tq, MQ_W), row),
            pl.BlockSpec((tq, IDXQ_W), row),
            pl.BlockSpec((tq, LANES), row),
            _resident((None, s_pad, KV_W), per_b),
            _resident((None, n_half, N_KV_HEADS, vrows, half), lambda b, i: (b, 0, 0, 0, 0)),
            _resident((None, s_pad, LANES), per_b),
            pl.BlockSpec((n_mem, MQ_W), lambda b, i: (b, 0)),
            pl.BlockSpec((n_mem, MQ_W), lambda b, i: (b, 0)),
        ],
        out_specs=pl.BlockSpec((tq, O_W), row),
        out_shape=jax.ShapeDtypeStruct((n, O_W), BF16),
        scratch_shapes=[pltpu.VMEM((n_kt, tk, tq), jnp.int32),
                        pltpu.VMEM((n_kt, tk, tq), jnp.int16),
                        pltpu.VMEM((n_kt, tk, tq), jnp.int16),
                        pltpu.VMEM((N_KV_HEADS, half, GROUP * tq), F32),
                        pltpu.VMEM((N_KV_HEADS, half, GROUP * tq), F32)],
        compiler_params=_params(2),
        name="dsa_attend",
    )(q, qm, qi, tail, k, vt, kid, mk, mv)


def _key_tiles_t(v, tk):
    nb, s, _ = v.shape
    vt = v.reshape(nb, s // tk, tk, N_KV_HEADS, HEAD_DIM).transpose(0, 1, 3, 4, 2)
    ones = jnp.ones((nb, s // tk, N_KV_HEADS, V_ONES, tk), v.dtype)
    return jnp.concatenate([vt, ones], axis=3)


def _band_table_kernel(rb_ref, o_ref, *, qc, band, prefix):
    h = pl.program_id(0)
    shape = o_ref.shape
    r = lax.broadcasted_iota(jnp.int32, shape, 0)
    w = lax.broadcasted_iota(jnp.int32, shape, 1)
    idx = jnp.clip(r - w + prefix, -REL_CLIP, REL_CLIP) + REL_CLIP
    lo = (r // qc) * qc
    inband = jnp.logical_and(w >= lo, w < lo + band)

    def body(t, acc):
        return jnp.where(idx == t, rb_ref[h, t], acc)

    acc = lax.fori_loop(0, 2 * REL_CLIP + 1, body, jnp.zeros(shape, F32))
    o_ref[...] = jnp.where(inband, acc, NEG_INF)


def _band_table(rel_bias, rows, width, qc, band, prefix):
    nh = rel_bias.shape[0]
    return pl.pallas_call(
        functools.partial(_band_table_kernel, qc=qc, band=band, prefix=prefix),
        grid=(nh,),
        in_specs=[pl.BlockSpec(memory_space=pltpu.SMEM)],
        out_specs=pl.BlockSpec((None, rows, width), lambda h: (h, 0, 0)),
        out_shape=jax.ShapeDtypeStruct((nh, rows, width), F32),
        compiler_params=_params(1),
        name="band_table",
    )(rel_bias)


def _band_kernel(q_ref, qm_ref, k_ref, v_ref, tab_ref, mk_ref, mv_ref, o_ref, *, tq, width, kp0):
    i = pl.program_id(1)
    off = pl.multiple_of(i * tq, tq)
    kp_ok = (kp0 + i * tq + lax.broadcasted_iota(jnp.int32, (GROUP * tq, width), 1)) >= 0
    for g in range(N_KV_HEADS):
        hs = slice(g * HEAD_DIM, (g + 1) * HEAD_DIM)
        q3 = _stack_heads(q_ref, g)
        s = _dot_nt(q3, k_ref[pl.ds(off, width), hs]) * ATTN_SCALE
        s = s + jnp.concatenate([tab_ref[GROUP * g + r] for r in range(GROUP)], axis=0)
        s = jnp.where(kp_ok, s, NEG_INF)
        m = jnp.max(s, axis=1, keepdims=True)
        p = jnp.exp(s - m)
        l = jnp.sum(p, axis=1, keepdims=True)
        o = _dot(p.astype(BF16), v_ref[pl.ds(off, width), hs]) / l
        _unstack_heads(o_ref, g, o, tq)
    _mem_attend(qm_ref, mk_ref, mv_ref, o_ref)


def _band(q, qm, k_all, v_all, table, mk, mv, *, tq, kp0):
    n = q.shape[0]
    nb, l_pad, _ = k_all.shape
    nq = n // nb // tq
    n_mem = mk.shape[0] // nb
    width = table.shape[2]
    row = lambda b, i: (b * nq + i, 0)
    per_b = lambda b, i: (b, 0, 0)
    return pl.pallas_call(
        functools.partial(_band_kernel, tq=tq, width=width, kp0=kp0),
        grid=(nb, nq),
        in_specs=[
            pl.BlockSpec((tq, Q_W), row),
            pl.BlockSpec((tq, MQ_W), row),
            _resident((None, l_pad, KV_W), per_b),
            _resident((None, l_pad, KV_W), per_b),
            _resident(table.shape, lambda b, i: (0, 0, 0)),
            pl.BlockSpec((n_mem, MQ_W), lambda b, i: (b, 0)),
            pl.BlockSpec((n_mem, MQ_W), lambda b, i: (b, 0)),
        ],
        out_specs=pl.BlockSpec((tq, O_W), row),
        out_shape=jax.ShapeDtypeStruct((n, O_W), BF16),
        compiler_params=_params(2),
        name="band_attend",
    )(q, qm, k_all, v_all, table, mk, mv)


def _pad_rows(a, rows):
    return jnp.pad(a, ((0, 0), (0, rows - a.shape[1]), (0, 0)))


def kernel(x_prompt, x_sample, mem_prompt, cache_a_k, cache_a_v, cache_a_kidx, cache_b_k, cache_b_v,
           cache_mem_k, cache_mem_v, w_in_a, w_in_b, w_o, rel_bias, w_mem_kv,
           w_ffn1_in, w_ffn1_out, w_ffn2_in, w_ffn2_out, ln_g, ln_b):
    depth = w_o.shape[0]
    alpha = (2 * depth) ** 0.25
    bp, tp, d = x_prompt.shape
    bs, ts, _ = x_sample.shape
    past = cache_a_k.shape[2]
    n_mem = mem_prompt.shape[1]
    assert tp % DSA_TK == 0 and ts % 16 == 0 and ts <= CHUNK
    assert cache_b_k.shape[2] == WIN_B and tp >= WIN_B

    pa_pad = -(-PROJ_A // LANES) * LANES
    wa = jnp.pad(w_in_a, ((0, 0), (0, 0), (0, pa_pad - PROJ_A))).astype(BF16)
    wb, wo = w_in_b.astype(BF16), w_o.astype(BF16)
    w1i, w1o = w_ffn1_in.astype(BF16), (0.5 * w_ffn1_out).astype(BF16)
    w2i, w2o = w_ffn2_in.astype(BF16), (0.5 * w_ffn2_out).astype(BF16)

    memf, memb = _mem_kv(mem_prompt.reshape(bp * n_mem, d), w_mem_kv.astype(BF16))
    mk_p = memf[..., :MQ_W].reshape(depth, bp, n_mem, N_MEM_HEADS, HEAD_DIM)
    mv_p = memf[..., MQ_W:].reshape(depth, bp, n_mem, N_MEM_HEADS, HEAD_DIM)
    mkb_p, mvb_p = memb[..., :MQ_W], memb[..., MQ_W:]
    mkb_s = cache_mem_k.astype(BF16).reshape(depth, bs * n_mem, MQ_W)
    mvb_s = cache_mem_v.astype(BF16).reshape(depth, bs * n_mem, MQ_W)

    tab_p = _rope_tables(jnp.arange(tp))
    tab_s = jnp.tile(_rope_tables(past + jnp.arange(ts)), (1, bs, 1))

    xp = x_prompt.reshape(bp * tp, d)
    xs = x_sample.reshape(bs * ts, d)
    st = {k: [] for k in ("akp", "avp", "aip", "bkp", "bvp", "aks", "avs", "ais", "bks", "bvs")}
    heads = lambda a, b, t: a.reshape(b, t, N_KV_HEADS, HEAD_DIM)

    for i in range(depth):
        j = i // 2
        g, b = ln_g[i], ln_b[i]
        xp = _ffn_ln(xp, w1i, w1o, i, g[0], b[0], alpha)
        xs = _ffn_ln(xs, w1i, w1o, i, g[0], b[0], alpha)
        if i % 2 == 0:
            q, kf, vf, kb, vb, qm, qi, tail, kid, vt = _proj(xp, wa, j, tab_p, True)
            ap = _dsa(q, qm, qi, tail, kb.reshape(bp, tp, KV_W),
                      vt.reshape((bp, tp // DSA_HALF) + vt.shape[1:]),
                      kid.reshape(bp, tp, LANES), mkb_p[i], mvb_p[i],
                      tq=min(TQ, tp), s_valid=tp, pos0=0, topk=min(TOPK_MAX, tp // 4))
            st["akp"].append(heads(kf, bp, tp))
            st["avp"].append(heads(vf, bp, tp))
            st["aip"].append(tail[:, :IDX_DIM].reshape(bp, tp, IDX_DIM))

            q, kf, vf, kb, vb, qm, qi, tail, kid = _proj(xs, wa, j, tab_s, True)
            tk = DSA_TK
            s_valid = past + ts
            s_pad = -(-s_valid // tk) * tk
            ci = cache_a_kidx[j].astype(BF16)
            k_all = _pad_rows(jnp.concatenate(
                [cache_a_k[j].astype(BF16).reshape(bs, past, KV_W), kb.reshape(bs, ts, KV_W)], 1), s_pad)
            v_all = _pad_rows(jnp.concatenate(
                [cache_a_v[j].astype(BF16).reshape(bs, past, KV_W), vb.reshape(bs, ts, KV_W)], 1), s_pad)
            kid_all = _pad_rows(jnp.concatenate(
                [jnp.concatenate([ci, ci], -1), kid.reshape(bs, ts, LANES)], 1), s_pad)
            tq = TQ_SAMPLE
            qpad = lambda a: _pad_rows(a.reshape(bs, ts, a.shape[-1]), tq).reshape(bs * tq, a.shape[-1])
            as_ = _dsa(qpad(q), qpad(qm), qpad(qi), qpad(tail), k_all, _key_tiles_t(v_all, tk // 2), kid_all,
                       mkb_s[i], mvb_s[i], tq=tq, s_valid=s_valid, pos0=past,
                       topk=min(TOPK_MAX, s_valid // 4))
            as_ = as_.reshape(bs, tq, O_W)[:, :ts].reshape(bs * ts, O_W)
            st["aks"].append(heads(kf, bs, ts))
            st["avs"].append(heads(vf, bs, ts))
            st["ais"].append(tail[:, :IDX_DIM].reshape(bs, ts, IDX_DIM))
        else:
            q, kf, vf, kb, vb, qm = _proj(xp, wb, j, None, False)
            tq = LANES
            table = _band_table(rel_bias[j], tq, tq + WIN_B, CHUNK, WIN_B + CHUNK, WIN_B)
            zpad = jnp.zeros((bp, WIN_B, KV_W), BF16)
            ap = _band(q, qm, jnp.concatenate([zpad, kb.reshape(bp, tp, KV_W)], 1),
                       jnp.concatenate([zpad, vb.reshape(bp, tp, KV_W)], 1),
                       table, mkb_p[i], mvb_p[i], tq=tq, kp0=-WIN_B)
            n_keep = min(WIN_B, tp)
            st["bkp"].append(heads(kf, bp, tp)[:, tp - n_keep:])
            st["bvp"].append(heads(vf, bp, tp)[:, tp - n_keep:])

            q, kf, vf, kb, vb, qm = _proj(xs, wb, j, None, False)
            width = -(-(WIN_B + ts) // LANES) * LANES
            table = _band_table(rel_bias[j], ts, width, ts, WIN_B + ts, WIN_B)
            k_all = _pad_rows(jnp.concatenate(
                [cache_b_k[j].astype(BF16).reshape(bs, WIN_B, KV_W), kb.reshape(bs, ts, KV_W)], 1), width)
            v_all = _pad_rows(jnp.concatenate(
                [cache_b_v[j].astype(BF16).reshape(bs, WIN_B, KV_W), vb.reshape(bs, ts, KV_W)], 1), width)
            as_ = _band(q, qm, k_all, v_all, table, mkb_s[i], mvb_s[i], tq=ts, kp0=past - WIN_B)
            st["bks"].append(jnp.concatenate([cache_b_k[j][:, ts:], heads(kf, bs, ts)], 1))
            st["bvs"].append(jnp.concatenate([cache_b_v[j][:, ts:], heads(vf, bs, ts)], 1))

        xp2 = _oproj_ln(ap, wo, i, xp, g[1], b[1], alpha)
        xs2 = _oproj_ln(as_, wo, i, xs, g[1], b[1], alpha)
        xp = _ffn_ln(xp2, w2i, w2o, i, g[2], b[2], alpha)
        xs = _ffn_ln(xs2, w2i, w2o, i, g[2], b[2], alpha)

    stack = lambda name: jnp.stack(st[name])
    return (xp.reshape(bp, tp, d), xs.reshape(bs, ts, d),
            stack("akp"), stack("avp"), stack("aip"), stack("bkp"), stack("bvp"), mk_p, mv_p,
            stack("aks"), stack("avs"), stack("ais"), stack("bks"), stack("bvs"))
```

```python
import functools

import numpy as np
import jax
import jax.numpy as jnp
from jax import lax
from jax.experimental import pallas as pl
from jax.experimental.pallas import tpu as pltpu

CHUNK = 64
CHUNK_SHIFT = 6
N_HEADS = 12
N_KV_HEADS = 4
HEAD_DIM = 128
GROUP = N_HEADS // N_KV_HEADS
N_MEM_HEADS = 4
IDX_HEADS = 8
IDX_DIM = 64
TOPK_MAX = 256
WIN_B = 8 * CHUNK
REL_CLIP = 128
ROPE_THETA = 10000.0
LN_EPS = 1e-5
Q_W = N_HEADS * HEAD_DIM
KV_W = N_KV_HEADS * HEAD_DIM
MQ_W = N_MEM_HEADS * HEAD_DIM
IDXQ_W = IDX_HEADS * IDX_DIM
O_W = Q_W + MQ_W
PROJ_A = Q_W + 2 * KV_W + MQ_W + IDXQ_W + IDX_DIM + IDX_HEADS
PROJ_B = Q_W + 2 * KV_W + MQ_W
ATTN_SCALE = HEAD_DIM ** -0.5
IDX_W_SCALE = (IDX_HEADS * IDX_DIM) ** -0.5
NEG_INF = -1e30
EXP2_SCALE = ATTN_SCALE * float(np.log2(np.e))

LANES = 128
FFN_TM = 1024
FFN_TF = 512
TQ = 2 * LANES
TQ_SAMPLE = LANES
DSA_TK = 4 * LANES
DSA_HALF = DSA_TK // 2
V_ONES = 16
MAX_PEEL_PASSES = 4.0
VMEM_LIMIT = 62 * 1024 * 1024

F32 = jnp.float32
BF16 = jnp.bfloat16
INT_MIN = -(2 ** 31)
NEG_INF_KEY = int(np.array(NEG_INF, np.float32).view(np.int32)) ^ 0x7FFFFFFF


def _dot(a, b):
    return jnp.dot(a, b, preferred_element_type=F32)


def _dot_nt(a, b):
    return lax.dot_general(a, b, (((1,), (1,)), ((), ())), preferred_element_type=F32)


def _params(n_axes):
    return pltpu.CompilerParams(dimension_semantics=("arbitrary",) * n_axes,
                                vmem_limit_bytes=VMEM_LIMIT)


def _resident(block_shape, index_map):
    return pl.BlockSpec(block_shape, index_map, pipeline_mode=pl.Buffered(1))


def _layer_norm(y, g, b):
    mu = jnp.mean(y, axis=-1, keepdims=True)
    d = y - mu
    var = jnp.mean(d * d, axis=-1, keepdims=True)
    return d * lax.rsqrt(var + LN_EPS) * g + b


def _ffn_ln_kernel(x_ref, wg_ref, wu_ref, wo_ref, g_ref, b_ref, o_ref, *, alpha):
    j = pl.program_id(1)

    @pl.when(j == 0)
    def _():
        o_ref[...] = alpha * x_ref[...]

    xb = x_ref[...].astype(BF16)
    gate = _dot(xb, wg_ref[...])
    up = _dot(xb, wu_ref[...])
    h = (gate / (1.0 + jnp.exp(-gate))) * up
    o_ref[...] += _dot(h.astype(BF16), wo_ref[...])

    @pl.when(j == pl.num_programs(1) - 1)
    def _():
        o_ref[...] = _layer_norm(o_ref[...], g_ref[...], b_ref[...])


def _ffn_ln(x, w_in, w_out, layer, g, b, alpha):
    n, d = x.shape
    dff = w_out.shape[1]
    tm = min(FFN_TM, n)
    tf = FFN_TF if dff % FFN_TF == 0 else LANES
    nj = dff // tf
    return pl.pallas_call(
        functools.partial(_ffn_ln_kernel, alpha=alpha),
        grid=(n // tm, nj),
        in_specs=[
            pl.BlockSpec((tm, d), lambda i, j: (i, 0)),
            pl.BlockSpec((None, d, tf), lambda i, j: (layer, 0, j)),
            pl.BlockSpec((None, d, tf), lambda i, j: (layer, 0, j + nj)),
            pl.BlockSpec((None, tf, d), lambda i, j: (layer, j, 0)),
            pl.BlockSpec((1, d), lambda i, j: (0, 0)),
            pl.BlockSpec((1, d), lambda i, j: (0, 0)),
        ],
        out_specs=pl.BlockSpec((tm, d), lambda i, j: (i, 0)),
        out_shape=jax.ShapeDtypeStruct((n, d), F32),
        compiler_params=_params(2),
        name="ffn_ln",
    )(x, w_in, w_in, w_out, g.reshape(1, d), b.reshape(1, d))


def _oproj_ln_kernel(a_ref, w_ref, x_ref, g_ref, b_ref, o_ref, *, alpha):
    y = alpha * x_ref[...] + _dot(a_ref[...], w_ref[...])
    o_ref[...] = _layer_norm(y, g_ref[...], b_ref[...])


def _oproj_ln(a, w, layer, x, g, b, alpha):
    n, d = x.shape
    ow = a.shape[1]
    tm = min(512, n)
    return pl.pallas_call(
        functools.partial(_oproj_ln_kernel, alpha=alpha),
        grid=(n // tm,),
        in_specs=[
            pl.BlockSpec((tm, ow), lambda i: (i, 0)),
            _resident((None, ow, d), lambda i: (layer, 0, 0)),
            pl.BlockSpec((tm, d), lambda i: (i, 0)),
            pl.BlockSpec((1, d), lambda i: (0, 0)),
            pl.BlockSpec((1, d), lambda i: (0, 0)),
        ],
        out_specs=pl.BlockSpec((tm, d), lambda i: (i, 0)),
        out_shape=jax.ShapeDtypeStruct((n, d), F32),
        compiler_params=_params(1),
        name="oproj_ln",
    )(a, w, x, g.reshape(1, d), b.reshape(1, d))


def _mem_kv_kernel(x_ref, w_ref, of_ref, ob_ref):
    y = _dot(x_ref[...].astype(BF16), w_ref[...])
    of_ref[...] = y
    ob_ref[...] = y.astype(BF16)


def _mem_kv(mem, w):
    n, d = mem.shape
    depth, _, c = w.shape
    return pl.pallas_call(
        _mem_kv_kernel,
        grid=(depth,),
        in_specs=[
            _resident((n, d), lambda l: (0, 0)),
            pl.BlockSpec((None, d, c), lambda l: (l, 0, 0)),
        ],
        out_specs=[
            pl.BlockSpec((None, n, c), lambda l: (l, 0, 0)),
            pl.BlockSpec((None, n, c), lambda l: (l, 0, 0)),
        ],
        out_shape=[jax.ShapeDtypeStruct((depth, n, c), F32),
                   jax.ShapeDtypeStruct((depth, n, c), BF16)],
        compiler_params=_params(1),
        name="mem_kv",
    )(mem, w)


def _rope_tables(pos):
    posf = pos.astype(F32)[:, None]

    def cs(half):
        inv_freq = ROPE_THETA ** (-jnp.arange(half, dtype=F32) / half)
        ang = posf * inv_freq[None, :]
        return jnp.cos(ang), jnp.sin(ang)

    c64, s64 = cs(HEAD_DIM // 2)
    c32, s32 = cs(IDX_DIM // 2)
    z32 = jnp.zeros_like(s32)
    cos128 = jnp.concatenate([c64, c64], -1)
    sin128 = jnp.concatenate([-s64, s64], -1)
    cos64 = jnp.concatenate([c32] * 4, -1)
    sa64 = jnp.concatenate([-s32, z32, -s32, z32], -1)
    sb64 = jnp.concatenate([z32, s32, z32, s32], -1)
    lane = jnp.arange(LANES)[None, :]
    first = lane < IDX_DIM
    return jnp.stack([cos128, sin128, cos64, sa64, sb64,
                      jnp.where(first, cos64, 1.0), jnp.where(first, sa64, 0.0),
                      jnp.where(first, sb64, 0.0)])


def _proj_kernel(x_ref, w_ref, *refs, layer_a, emit_vt):
    xb = x_ref[...].astype(BF16)
    vt_ref = None
    if layer_a:
        tab_ref, q_ref, kf_ref, vf_ref, kb_ref, vb_ref, qm_ref, qi_ref, tail_ref, kid_ref = refs[:10]
        if emit_vt:
            vt_ref = refs[10]
    else:
        q_ref, kf_ref, vf_ref, kb_ref, vb_ref, qm_ref = refs

    def rope128(y):
        return y * tab_ref[0] + pltpu.roll(y, HEAD_DIM // 2, axis=1) * tab_ref[1]

    def rope64(y, t0):
        return (y * tab_ref[t0] + pltpu.roll(y, LANES - IDX_DIM // 2, axis=1) * tab_ref[t0 + 1]
                + pltpu.roll(y, IDX_DIM // 2, axis=1) * tab_ref[t0 + 2])

    def section(col0, width, fn):
        step = 512
        for c0 in range(0, width, step):
            cw = min(step, width - c0)
            y = _dot(xb, w_ref[:, col0 + c0:col0 + c0 + cw])
            for s in range(cw // LANES):
                fn(c0 + s * LANES, y[:, s * LANES:(s + 1) * LANES])

    def put(ref, fn=None):
        def f(off, y):
            ref[:, off:off + LANES] = (y if fn is None else fn(y)).astype(ref.dtype)
        return f

    def put2(ref_f, ref_b, fn=None):
        def f(off, y):
            y = y if fn is None else fn(y)
            ref_f[:, off // HEAD_DIM, :] = y
            ref_b[:, off:off + LANES] = y.astype(BF16)
        return f

    put_v = put2(vf_ref, vb_ref)

    def put_v_and_vt(off, y):
        put_v(off, y)
        g = off // HEAD_DIM
        for u in range(y.shape[0] // DSA_HALF):
            vt_ref[u, g, :HEAD_DIM, :] = y[u * DSA_HALF:(u + 1) * DSA_HALF, :].T.astype(BF16)
            vt_ref[u, g, HEAD_DIM:, :] = jnp.ones((V_ONES, DSA_HALF), BF16)

    rq = rope128 if layer_a else None
    section(0, Q_W, put(q_ref, rq))
    section(Q_W, KV_W, put2(kf_ref, kb_ref, rq))
    section(Q_W + KV_W, KV_W, put_v_and_vt if emit_vt else put_v)
    section(Q_W + 2 * KV_W, MQ_W, put(qm_ref))
    if layer_a:
        c = Q_W + 2 * KV_W + MQ_W
        section(c, IDXQ_W, put(qi_ref, lambda y: rope64(y, 2)))

        def tail(off, y):
            t = rope64(y, 5)
            tail_ref[...] = t
            lane = lax.broadcasted_iota(jnp.int32, t.shape, 1)
            dup = jnp.where(lane < IDX_DIM, t, pltpu.roll(t, IDX_DIM, axis=1))
            kid_ref[...] = dup.astype(BF16)
        section(c + IDXQ_W, LANES, tail)


def _proj(x, w, layer, tab, layer_a):
    n, d = x.shape
    pw = w.shape[2]
    tm = min(512, n)
    row = lambda i: (i, 0)
    in_specs = [pl.BlockSpec((tm, d), row), _resident((None, d, pw), lambda i: (layer, 0, 0))]
    args = [x, w]
    kv_state = (N_KV_HEADS, HEAD_DIM)
    outs = [((Q_W,), BF16), (kv_state, F32), (kv_state, F32), ((KV_W,), BF16), ((KV_W,), BF16),
            ((MQ_W,), BF16)]
    if layer_a:
        nt = tab.shape[1] // tm
        in_specs.append(pl.BlockSpec((8, tm, LANES), lambda i: (0, i % nt, 0)))
        args.append(tab)
        outs += [((IDXQ_W,), BF16), ((LANES,), F32), ((LANES,), BF16)]
    emit_vt = layer_a and tm % DSA_HALF == 0
    out_specs = [pl.BlockSpec((tm,) + wd, lambda i, nd=len(wd): (i,) + (0,) * nd) for wd, _ in outs]
    out_shape = [jax.ShapeDtypeStruct((n,) + wd, dt) for wd, dt in outs]
    if emit_vt:
        vt_tile = (N_KV_HEADS, HEAD_DIM + V_ONES, DSA_HALF)
        out_specs.append(pl.BlockSpec((tm // DSA_HALF,) + vt_tile, lambda i: (i, 0, 0, 0)))
        out_shape.append(jax.ShapeDtypeStruct((n // DSA_HALF,) + vt_tile, BF16))
    return pl.pallas_call(
        functools.partial(_proj_kernel, layer_a=layer_a, emit_vt=emit_vt),
        grid=(n // tm,),
        in_specs=in_specs,
        out_specs=out_specs,
        out_shape=out_shape,
        compiler_params=_params(1),
        name="proj_a" if layer_a else "proj_b",
    )(*args)


def _stack_heads(q_ref, g):
    return jnp.concatenate(
        [q_ref[:, (GROUP * g + r) * HEAD_DIM:(GROUP * g + r + 1) * HEAD_DIM] for r in range(GROUP)],
        axis=0)


def _unstack_heads(o_ref, g, o, tq):
    for r in range(GROUP):
        h = GROUP * g + r
        o_ref[:, h * HEAD_DIM:(h + 1) * HEAD_DIM] = o[r * tq:(r + 1) * tq].astype(o_ref.dtype)


def _mem_attend(qm_ref, mk_ref, mv_ref, o_ref):
    for hm in range(N_MEM_HEADS):
        sl = slice(hm * HEAD_DIM, (hm + 1) * HEAD_DIM)
        s = _dot_nt(qm_ref[:, sl], mk_ref[:, sl]) * ATTN_SCALE
        m = jnp.max(s, axis=1, keepdims=True)
        p = jnp.exp(s - m)
        l = jnp.sum(p, axis=1, keepdims=True)
        om = _dot(p.astype(BF16), mv_ref[:, sl]) / l
        o_ref[:, Q_W + hm * HEAD_DIM:Q_W + (hm + 1) * HEAD_DIM] = om.astype(o_ref.dtype)


def _sort_key(x):
    bits = lax.bitcast_convert_type(x, jnp.int32)
    return bits ^ ((bits >> 31) & 0x7FFFFFFF)


def _dsa_kernel(q_ref, qm_ref, qi_ref, tail_ref, k_ref, vt_ref, kid_ref, mk_ref, mv_ref,
                o_ref, keys_ref, hi_ref, lo_ref, sa_ref, sb_ref, *, tq, tk, s_valid, pos0, topk):
    i = pl.program_id(1)
    row0 = pos0 + i * tq
    qpos = row0 + lax.broadcasted_iota(jnp.int32, (1, tq), 1)
    lim = jnp.minimum(((qpos >> CHUNK_SHIFT) + 1) * CHUNK, s_valid)
    lim_max = jnp.minimum((((row0 + tq - 1) >> CHUNK_SHIFT) + 1) * CHUNK, s_valid)
    n_kt = (lim_max + tk - 1) // tk

    lane = lax.broadcasted_iota(jnp.int32, (tq, LANES), 1)
    pairs = []
    for p in range(IDX_HEADS // 2):
        qp = qi_ref[:, p * LANES:(p + 1) * LANES]
        zero = jnp.zeros_like(qp)
        pairs.append(jnp.concatenate(
            [jnp.where(lane < IDX_DIM, qp, zero), jnp.where(lane >= IDX_DIM, qp, zero)], axis=0))
    wt = tail_ref[...].T[IDX_DIM:IDX_DIM + IDX_HEADS, :] * IDX_W_SCALE
    krow = lax.broadcasted_iota(jnp.int32, (tk, tq), 0)
    n_keys = keys_ref.shape[0] * tk

    def idx_body(kt, carry):
        kid = kid_ref[pl.ds(pl.multiple_of(kt * tk, tk), tk), :]
        acc = jnp.zeros((tk, tq), F32)
        for p in range(IDX_HEADS // 2):
            rel = jnp.maximum(_dot_nt(kid, pairs[p]), 0.0)
            acc = acc + wt[2 * p:2 * p + 1, :] * rel[:, :tq]
            acc = acc + wt[2 * p + 1:2 * p + 2, :] * rel[:, tq:]
        sidx = krow + kt * tk
        score = jnp.where(sidx < lim, acc, NEG_INF)
        key = jnp.where(score == 0.0, n_keys - sidx, _sort_key(score))
        keys_ref[kt] = key
        hi_ref[kt] = (key >> 16).astype(jnp.int16)
        lo_ref[kt] = (key ^ 0x8000).astype(jnp.int16)
        return carry

    lax.fori_loop(0, n_kt, idx_body, 0)

    n_part = 4
    sub16 = 16

    def count16(ref, pred, cand):
        candb = jnp.broadcast_to(cand.astype(jnp.int16), (sub16, tq))

        def body(kt, cnts):
            xx = ref[kt]
            cnts = list(cnts)
            for r in range(tk // sub16):
                hit = pred(xx[r * sub16:(r + 1) * sub16, :], candb).astype(jnp.int16)
                cnts[r % n_part] = cnts[r % n_part] + hit
            return tuple(cnts)

        zero = jnp.zeros((sub16, tq), jnp.int16)
        cnts = lax.fori_loop(0, n_kt, body, (zero,) * n_part)
        total = sum(c.astype(F32) for c in cnts)
        return jnp.sum(total, axis=0, keepdims=True)

    ge = lambda a, b: a >= b
    gt = lambda a, b: a > b
    i16_min = -(2 ** 15)

    def search16(ref, need):
        zeros = jnp.zeros((1, tq), jnp.int32)
        thr0 = jnp.where(count16(ref, ge, zeros) >= need, zeros, jnp.full_like(zeros, i16_min))

        def step(it, thr):
            cand = thr | lax.shift_left(jnp.int32(1), 14 - it)
            return jnp.where(count16(ref, ge, cand) >= need, cand, thr)

        return lax.fori_loop(0, 15, step, thr0)

    need = jnp.full((1, tq), float(topk), F32)
    thr_hi = search16(hi_ref, need)
    need_lo = need - count16(hi_ref, gt, thr_hi)

    thr_hib = jnp.broadcast_to(thr_hi.astype(jnp.int16), (tk, tq))

    def keep_lo(kt, carry):
        lo_ref[kt] = jnp.where(hi_ref[kt] == thr_hib, lo_ref[kt], jnp.int16(i16_min))
        return carry

    lax.fori_loop(0, n_kt, keep_lo, 0)
    thr_lo = search16(lo_ref, need_lo)
    thr = (thr_hi << 16) | ((thr_lo ^ 0x8000) & 0xFFFF)
    thr_b = jnp.broadcast_to(thr, (tk, tq))

    def count32(pred):
        def body(kt, cnt):
            hit = jnp.where(pred(keys_ref[kt], kt * tk), 1, 0)
            for r in range(tk // 8):
                cnt = cnt + hit[r * 8:(r + 1) * 8, :]
            return cnt
        cnt = lax.fori_loop(0, n_kt, body, jnp.zeros((8, tq), jnp.int32))
        return jnp.sum(cnt.astype(F32), axis=0, keepdims=True)

    surplus = jnp.where(thr > NEG_INF_KEY, count32(lambda kk, k0: kk >= thr_b) - need, 0.0)

    max_surplus = jnp.max(surplus)

    @pl.when(max_surplus > 0.0)
    def _():
        idx_bits = (keys_ref.shape[0] * tk).bit_length()

        def by_peeling():
            def peel(j, bound):
                bound_b = jnp.broadcast_to(bound, (tk, tq))

                def body(kt, best):
                    idx = krow + kt * tk
                    tie = jnp.logical_and(keys_ref[kt] == thr_b, idx < bound_b)
                    cand = jnp.where(tie, idx, -1)
                    for r in range(tk // 8):
                        best = jnp.maximum(best, cand[r * 8:(r + 1) * 8, :])
                    return best

                best = lax.fori_loop(0, n_kt, body, jnp.full((8, tq), -1, jnp.int32))
                top = jnp.max(best.astype(F32), axis=0, keepdims=True).astype(jnp.int32)
                return jnp.where(surplus > j.astype(F32), top, bound)

            return lax.fori_loop(0, max_surplus.astype(jnp.int32), peel,
                                 jnp.full((1, tq), 2 ** idx_bits, jnp.int32))

        def by_bisection():
            keep = count32(lambda kk, k0: kk == thr_b) - surplus

            def step(it, bound):
                cand = bound | lax.shift_left(jnp.int32(1), idx_bits - 1 - it)
                cand_b = jnp.broadcast_to(cand, (tk, tq))
                below = count32(lambda kk, k0: jnp.logical_and(kk == thr_b, krow + k0 < cand_b))
                return jnp.where(below <= keep, cand, bound)

            return lax.fori_loop(0, idx_bits, step, jnp.zeros((1, tq), jnp.int32))

        bound = lax.cond(max_surplus <= MAX_PEEL_PASSES, by_peeling, by_bisection)
        bound_b = jnp.broadcast_to(bound, (tk, tq))

        def demote(kt, carry):
            kk = keys_ref[kt]
            drop = jnp.logical_and(kk == thr_b, krow + kt * tk >= bound_b)
            keys_ref[kt] = jnp.where(drop, thr_b - 1, kk)
            return carry

        lax.fori_loop(0, n_kt, demote, 0)

    thr = jnp.maximum(thr, NEG_INF_KEY + 1)

    q3 = [_stack_heads(q_ref, g) for g in range(N_KV_HEADS)]

    half = tk // 2
    thr_h = jnp.broadcast_to(thr, (half, tq))

    def qk_stage(kt, sub, buf_ref):
        off = pl.multiple_of(kt * tk + sub * half, half)
        kk = keys_ref[kt, sub * half:(sub + 1) * half, :]
        bias = jnp.where(kk >= thr_h, 0.0, NEG_INF)
        bias = jnp.concatenate([bias] * GROUP, axis=1)
        for g in range(N_KV_HEADS):
            hs = slice(g * HEAD_DIM, (g + 1) * HEAD_DIM)
            buf_ref[g] = _dot_nt(k_ref[pl.ds(off, half), hs], q3[g]) + bias

    def softmax_stage(kt, sub, buf_ref, carry):
        out = []
        for g in range(N_KV_HEADS):
            m, acc = carry[g]
            m_new = jnp.maximum(m, jnp.max(buf_ref[g], axis=0, keepdims=True))
            a = jnp.exp2((m - m_new) * EXP2_SCALE)
            p = jnp.exp2((buf_ref[g] - m_new) * EXP2_SCALE)
            acc = a * acc + _dot(vt_ref[2 * kt + sub, g], p.astype(BF16))
            out.append((m_new, acc))
        return tuple(out)

    def att_body(kt, carry):
        qk_stage(kt, 1, sb_ref)
        carry = softmax_stage(kt, 0, sa_ref, carry)
        qk_stage(jnp.minimum(kt + 1, n_kt - 1), 0, sa_ref)
        return softmax_stage(kt, 1, sb_ref, carry)

    init = tuple((jnp.full((1, GROUP * tq), NEG_INF, F32),
                  jnp.zeros((HEAD_DIM + V_ONES, GROUP * tq), F32)) for _ in range(N_KV_HEADS))
    qk_stage(0, 0, sa_ref)
    res = lax.fori_loop(0, n_kt, att_body, init)
    for g in range(N_KV_HEADS):
        acc = res[g][1]
        o_t = acc[:HEAD_DIM] / acc[HEAD_DIM:HEAD_DIM + 1]
        for r in range(GROUP):
            h = GROUP * g + r
            o_ref[:, h * HEAD_DIM:(h + 1) * HEAD_DIM] = o_t[:, r * tq:(r + 1) * tq].T.astype(o_ref.dtype)

    _mem_attend(qm_ref, mk_ref, mv_ref, o_ref)


def _dsa(q, qm, qi, tail, k, vt, kid, mk, mv, *, tq, s_valid, pos0, topk):
    n = q.shape[0]
    nb, n_half, _, vrows, half = vt.shape
    n_kt, tk = n_half // 2, 2 * half
    s_pad = k.shape[1]
    nq = n // nb // tq
    n_mem = mk.shape[0] // nb
    row = lambda b, i: (b * nq + i, 0)
    per_b = lambda b, i: (b, 0, 0)
    return pl.pallas_call(
        functools.partial(_dsa_kernel, tq=tq, tk=tk, s_valid=s_valid, pos0=pos0, topk=topk),
        grid=(nb, nq),
        in_specs=[
            pl.BlockSpec((tq, Q_W), row),
            pl.BlockSpec((tq, MQ_W), row),
            pl.BlockSpec((tq, IDXQ_W), row),
            pl.BlockSpec((tq, LANES), row),
            _resident((None, s_pad, KV_W), per_b),
            _resident((None, n_half, N_KV_HEADS, vrows, half), lambda b, i: (b, 0, 0, 0, 0)),
            _resident((None, s_pad, LANES), per_b),
            pl.BlockSpec((n_mem, MQ_W), lambda b, i: (b, 0)),
            pl.BlockSpec((n_mem, MQ_W), lambda b, i: (b, 0)),
        ],
        out_specs=pl.BlockSpec((tq, O_W), row),
        out_shape=jax.ShapeDtypeStruct((n, O_W), BF16),
        scratch_shapes=[pltpu.VMEM((n_kt, tk, tq), jnp.int32),
                        pltpu.VMEM((n_kt, tk, tq), jnp.int16),
                        pltpu.VMEM((n_kt, tk, tq), jnp.int16),
                        pltpu.VMEM((N_KV_HEADS, half, GROUP * tq), F32),
                        pltpu.VMEM((N_KV_HEADS, half, GROUP * tq), F32)],
        compiler_params=_params(2),
        name="dsa_attend",
    )(q, qm, qi, tail, k, vt, kid, mk, mv)


def _key_tiles_t(v, tk):
    nb, s, _ = v.shape
    vt = v.reshape(nb, s // tk, tk, N_KV_HEADS, HEAD_DIM).transpose(0, 1, 3, 4, 2)
    ones = jnp.ones((nb, s // tk, N_KV_HEADS, V_ONES, tk), v.dtype)
    return jnp.concatenate([vt, ones], axis=3)


def _band_table_kernel(rb_ref, o_ref, *, qc, band, prefix):
    h = pl.program_id(0)
    shape = o_ref.shape
    r = lax.broadcasted_iota(jnp.int32, shape, 0)
    w = lax.broadcasted_iota(jnp.int32, shape, 1)
    idx = jnp.clip(r - w + prefix, -REL_CLIP, REL_CLIP) + REL_CLIP
    lo = (r // qc) * qc
    inband = jnp.logical_and(w >= lo, w < lo + band)

    def body(t, acc):
        return jnp.where(idx == t, rb_ref[h, t], acc)

    acc = lax.fori_loop(0, 2 * REL_CLIP + 1, body, jnp.zeros(shape, F32))
    o_ref[...] = jnp.where(inband, acc, NEG_INF)


def _band_table(rel_bias, rows, width, qc, band, prefix):
    nh = rel_bias.shape[0]
    return pl.pallas_call(
        functools.partial(_band_table_kernel, qc=qc, band=band, prefix=prefix),
        grid=(nh,),
        in_specs=[pl.BlockSpec(memory_space=pltpu.SMEM)],
        out_specs=pl.BlockSpec((None, rows, width), lambda h: (h, 0, 0)),
        out_shape=jax.ShapeDtypeStruct((nh, rows, width), F32),
        compiler_params=_params(1),
        name="band_table",
    )(rel_bias)


def _band_kernel(q_ref, qm_ref, k_ref, v_ref, tab_ref, mk_ref, mv_ref, o_ref, *, tq, width, kp0):
    i = pl.program_id(1)
    off = pl.multiple_of(i * tq, tq)
    kp_ok = (kp0 + i * tq + lax.broadcasted_iota(jnp.int32, (GROUP * tq, width), 1)) >= 0
    for g in range(N_KV_HEADS):
        hs = slice(g * HEAD_DIM, (g + 1) * HEAD_DIM)
        q3 = _stack_heads(q_ref, g)
        s = _dot_nt(q3, k_ref[pl.ds(off, width), hs]) * ATTN_SCALE
        s = s + jnp.concatenate([tab_ref[GROUP * g + r] for r in range(GROUP)], axis=0)
        s = jnp.where(kp_ok, s, NEG_INF)
        m = jnp.max(s, axis=1, keepdims=True)
        p = jnp.exp(s - m)
        l = jnp.sum(p, axis=1, keepdims=True)
        o = _dot(p.astype(BF16), v_ref[pl.ds(off, width), hs]) / l
        _unstack_heads(o_ref, g, o, tq)
    _mem_attend(qm_ref, mk_ref, mv_ref, o_ref)


def _band(q, qm, k_all, v_all, table, mk, mv, *, tq, kp0):
    n = q.shape[0]
    nb, l_pad, _ = k_all.shape
    nq = n // nb // tq
    n_mem = mk.shape[0] // nb
    width = table.shape[2]
    row = lambda b, i: (b * nq + i, 0)
    per_b = lambda b, i: (b, 0, 0)
    return pl.pallas_call(
        functools.partial(_band_kernel, tq=tq, width=width, kp0=kp0),
        grid=(nb, nq),
        in_specs=[
            pl.BlockSpec((tq, Q_W), row),
            pl.BlockSpec((tq, MQ_W), row),
            _resident((None, l_pad, KV_W), per_b),
            _resident((None, l_pad, KV_W), per_b),
            _resident(table.shape, lambda b, i: (0, 0, 0)),
            pl.BlockSpec((n_mem, MQ_W), lambda b, i: (b, 0)),
            pl.BlockSpec((n_mem, MQ_W), lambda b, i: (b, 0)),
        ],
        out_specs=pl.BlockSpec((tq, O_W), row),
        out_shape=jax.ShapeDtypeStruct((n, O_W), BF16),
        compiler_params=_params(2),
        name="band_attend",
    )(q, qm, k_all, v_all, table, mk, mv)


def _pad_rows(a, rows):
    return jnp.pad(a, ((0, 0), (0, rows - a.shape[1]), (0, 0)))


def kernel(x_prompt, x_sample, mem_prompt, cache_a_k, cache_a_v, cache_a_kidx, cache_b_k, cache_b_v,
           cache_mem_k, cache_mem_v, w_in_a, w_in_b, w_o, rel_bias, w_mem_kv,
           w_ffn1_in, w_ffn1_out, w_ffn2_in, w_ffn2_out, ln_g, ln_b):
    depth = w_o.shape[0]
    alpha = (2 * depth) ** 0.25
    bp, tp, d = x_prompt.shape
    bs, ts, _ = x_sample.shape
    past = cache_a_k.shape[2]
    n_mem = mem_prompt.shape[1]
    assert tp % DSA_TK == 0 and ts % 16 == 0 and ts <= CHUNK
    assert cache_b_k.shape[2] == WIN_B and tp >= WIN_B

    pa_pad = -(-PROJ_A // LANES) * LANES
    wa = jnp.pad(w_in_a, ((0, 0), (0, 0), (0, pa_pad - PROJ_A))).astype(BF16)
    wb, wo = w_in_b.astype(BF16), w_o.astype(BF16)
    w1i, w1o = w_ffn1_in.astype(BF16), (0.5 * w_ffn1_out).astype(BF16)
    w2i, w2o = w_ffn2_in.astype(BF16), (0.5 * w_ffn2_out).astype(BF16)

    memf, memb = _mem_kv(mem_prompt.reshape(bp * n_mem, d), w_mem_kv.astype(BF16))
    mk_p = memf[..., :MQ_W].reshape(depth, bp, n_mem, N_MEM_HEADS, HEAD_DIM)
    mv_p = memf[..., MQ_W:].reshape(depth, bp, n_mem, N_MEM_HEADS, HEAD_DIM)
    mkb_p, mvb_p = memb[..., :MQ_W], memb[..., MQ_W:]
    mkb_s = cache_mem_k.astype(BF16).reshape(depth, bs * n_mem, MQ_W)
    mvb_s = cache_mem_v.astype(BF16).reshape(depth, bs * n_mem, MQ_W)

    tab_p = _rope_tables(jnp.arange(tp))
    tab_s = jnp.tile(_rope_tables(past + jnp.arange(ts)), (1, bs, 1))

    xp = x_prompt.reshape(bp * tp, d)
    xs = x_sample.reshape(bs * ts, d)
    st = {k: [] for k in ("akp", "avp", "aip", "bkp", "bvp", "aks", "avs", "ais", "bks", "bvs")}
    heads = lambda a, b, t: a.reshape(b, t, N_KV_HEADS, HEAD_DIM)

    for i in range(depth):
        j = i // 2
        g, b = ln_g[i], ln_b[i]
        xp = _ffn_ln(xp, w1i, w1o, i, g[0], b[0], alpha)
        xs = _ffn_ln(xs, w1i, w1o, i, g[0], b[0], alpha)
        if i % 2 == 0:
            q, kf, vf, kb, vb, qm, qi, tail, kid, vt = _proj(xp, wa, j, tab_p, True)
            ap = _dsa(q, qm, qi, tail, kb.reshape(bp, tp, KV_W),
                      vt.reshape((bp, tp // DSA_HALF) + vt.shape[1:]),
                      kid.reshape(bp, tp, LANES), mkb_p[i], mvb_p[i],
                      tq=min(TQ, tp), s_valid=tp, pos0=0, topk=min(TOPK_MAX, tp // 4))
            st["akp"].append(heads(kf, bp, tp))
            st["avp"].append(heads(vf, bp, tp))
            st["aip"].append(tail[:, :IDX_DIM].reshape(bp, tp, IDX_DIM))

            q, kf, vf, kb, vb, qm, qi, tail, kid = _proj(xs, wa, j, tab_s, True)
            tk = DSA_TK
            s_valid = past + ts
            s_pad = -(-s_valid // tk) * tk
            ci = cache_a_kidx[j].astype(BF16)
            k_all = _pad_rows(jnp.concatenate(
                [cache_a_k[j].astype(BF16).reshape(bs, past, KV_W), kb.reshape(bs, ts, KV_W)], 1), s_pad)
            v_all = _pad_rows(jnp.concatenate(
                [cache_a_v[j].astype(BF16).reshape(bs, past, KV_W), vb.reshape(bs, ts, KV_W)], 1), s_pad)
            kid_all = _pad_rows(jnp.concatenate(
                [jnp.concatenate([ci, ci], -1), kid.reshape(bs, ts, LANES)], 1), s_pad)
            tq = TQ_SAMPLE
            qpad = lambda a: _pad_rows(a.reshape(bs, ts, a.shape[-1]), tq).reshape(bs * tq, a.shape[-1])
            as_ = _dsa(qpad(q), qpad(qm), qpad(qi), qpad(tail), k_all, _key_tiles_t(v_all, tk // 2), kid_all,
                       mkb_s[i], mvb_s[i], tq=tq, s_valid=s_valid, pos0=past,
                       topk=min(TOPK_MAX, s_valid // 4))
            as_ = as_.reshape(bs, tq, O_W)[:, :ts].reshape(bs * ts, O_W)
            st["aks"].append(heads(kf, bs, ts))
            st["avs"].append(heads(vf, bs, ts))
            st["ais"].append(tail[:, :IDX_DIM].reshape(bs, ts, IDX_DIM))
        else:
            q, kf, vf, kb, vb, qm = _proj(xp, wb, j, None, False)
            tq = LANES
            table = _band_table(rel_bias[j], tq, tq + WIN_B, CHUNK, WIN_B + CHUNK, WIN_B)
            zpad = jnp.zeros((bp, WIN_B, KV_W), BF16)
            ap = _band(q, qm, jnp.concatenate([zpad, kb.reshape(bp, tp, KV_W)], 1),
                       jnp.concatenate([zpad, vb.reshape(bp, tp, KV_W)], 1),
                       table, mkb_p[i], mvb_p[i], tq=tq, kp0=-WIN_B)
            n_keep = min(WIN_B, tp)
            st["bkp"].append(heads(kf, bp, tp)[:, tp - n_keep:])
            st["bvp"].append(heads(vf, bp, tp)[:, tp - n_keep:])

            q, kf, vf, kb, vb, qm = _proj(xs, wb, j, None, False)
            width = -(-(WIN_B + ts) // LANES) * LANES
            table = _band_table(rel_bias[j], ts, width, ts, WIN_B + ts, WIN_B)
            k_all = _pad_rows(jnp.concatenate(
                [cache_b_k[j].astype(BF16).reshape(bs, WIN_B, KV_W), kb.reshape(bs, ts, KV_W)], 1), width)
            v_all = _pad_rows(jnp.concatenate(
                [cache_b_v[j].astype(BF16).reshape(bs, WIN_B, KV_W), vb.reshape(bs, ts, KV_W)], 1), width)
            as_ = _band(q, qm, k_all, v_all, table, mkb_s[i], mvb_s[i], tq=ts, kp0=past - WIN_B)
            st["bks"].append(jnp.concatenate([cache_b_k[j][:, ts:], heads(kf, bs, ts)], 1))
            st["bvs"].append(jnp.concatenate([cache_b_v[j][:, ts:], heads(vf, bs, ts)], 1))

        xp2 = _oproj_ln(ap, wo, i, xp, g[1], b[1], alpha)
        xs2 = _oproj_ln(as_, wo, i, xs, g[1], b[1], alpha)
        xp = _ffn_ln(xp2, w2i, w2o, i, g[2], b[2], alpha)
        xs = _ffn_ln(xs2, w2i, w2o, i, g[2], b[2], alpha)

    stack = lambda name: jnp.stack(st[name])
    return (xp.reshape(bp, tp, d), xs.reshape(bs, ts, d),
            stack("akp"), stack("avp"), stack("aip"), stack("bkp"), stack("bvp"), mk_p, mv_p,
            stack("aks"), stack("avs"), stack("ais"), stack("bks"), stack("bvs"))
```

```python
import functools

import numpy as np
import jax
import jax.numpy as jnp
from jax import lax
from jax.experimental import pallas as pl
from jax.experimental.pallas import tpu as pltpu

CHUNK = 64
CHUNK_SHIFT = 6
N_HEADS = 12
N_KV_HEADS = 4
HEAD_DIM = 128
GROUP = N_HEADS // N_KV_HEADS
N_MEM_HEADS = 4
IDX_HEADS = 8
IDX_DIM = 64
TOPK_MAX = 256
WIN_B = 8 * CHUNK
REL_CLIP = 128
ROPE_THETA = 10000.0
LN_EPS = 1e-5
Q_W = N_HEADS * HEAD_DIM
KV_W = N_KV_HEADS * HEAD_DIM
MQ_W = N_MEM_HEADS * HEAD_DIM
IDXQ_W = IDX_HEADS * IDX_DIM
O_W = Q_W + MQ_W
PROJ_A = Q_W + 2 * KV_W + MQ_W + IDXQ_W + IDX_DIM + IDX_HEADS
PROJ_B = Q_W + 2 * KV_W + MQ_W
ATTN_SCALE = HEAD_DIM ** -0.5
IDX_W_SCALE = (IDX_HEADS * IDX_DIM) ** -0.5
NEG_INF = -1e30
EXP2_SCALE = ATTN_SCALE * float(np.log2(np.e))

LANES = 128
FFN_TM = 1024
FFN_TF = 512
TQ = 2 * LANES
TQ_SAMPLE = LANES
DSA_TK = 8 * LANES
DSA_HALF = DSA_TK // 2
V_ONES = 16
MAX_PEEL_PASSES = 4.0
VMEM_LIMIT = 127 * 512 * 1024

F32 = jnp.float32
BF16 = jnp.bfloat16
INT_MIN = -(2 ** 31)
NEG_INF_KEY = int(np.array(NEG_INF, np.float32).view(np.int32)) ^ 0x7FFFFFFF


def _dot(a, b):
    return jnp.dot(a, b, preferred_element_type=F32)


def _dot_nt(a, b):
    return lax.dot_general(a, b, (((1,), (1,)), ((), ())), preferred_element_type=F32)


def _params(n_axes):
    return pltpu.CompilerParams(dimension_semantics=("arbitrary",) * n_axes,
                                vmem_limit_bytes=VMEM_LIMIT)


def _resident(block_shape, index_map):
    return pl.BlockSpec(block_shape, index_map, pipeline_mode=pl.Buffered(1))


def _layer_norm(y, g, b):
    mu = jnp.mean(y, axis=-1, keepdims=True)
    d = y - mu
    var = jnp.mean(d * d, axis=-1, keepdims=True)
    return d * lax.rsqrt(var + LN_EPS) * g + b


def _ffn_ln_kernel(x_ref, wg_ref, wu_ref, wo_ref, g_ref, b_ref, o_ref, *, alpha):
    j = pl.program_id(1)

    @pl.when(j == 0)
    def _():
        o_ref[...] = alpha * x_ref[...]

    xb = x_ref[...].astype(BF16)
    gate = _dot(xb, wg_ref[...])
    up = _dot(xb, wu_ref[...])
    h = (gate / (1.0 + jnp.exp(-gate))) * up
    o_ref[...] += _dot(h.astype(BF16), wo_ref[...])

    @pl.when(j == pl.num_programs(1) - 1)
    def _():
        o_ref[...] = _layer_norm(o_ref[...], g_ref[...], b_ref[...])


def _ffn_ln(x, w_in, w_out, layer, g, b, alpha):
    n, d = x.shape
    dff = w_out.shape[1]
    tm = min(FFN_TM, n)
    tf = FFN_TF if dff % FFN_TF == 0 else LANES
    nj = dff // tf
    return pl.pallas_call(
        functools.partial(_ffn_ln_kernel, alpha=alpha),
        grid=(n // tm, nj),
        in_specs=[
            pl.BlockSpec((tm, d), lambda i, j: (i, 0)),
            pl.BlockSpec((None, d, tf), lambda i, j: (layer, 0, j)),
            pl.BlockSpec((None, d, tf), lambda i, j: (layer, 0, j + nj)),
            pl.BlockSpec((None, tf, d), lambda i, j: (layer, j, 0)),
            pl.BlockSpec((1, d), lambda i, j: (0, 0)),
            pl.BlockSpec((1, d), lambda i, j: (0, 0)),
        ],
        out_specs=pl.BlockSpec((tm, d), lambda i, j: (i, 0)),
        out_shape=jax.ShapeDtypeStruct((n, d), F32),
        compiler_params=_params(2),
        name="ffn_ln",
    )(x, w_in, w_in, w_out, g.reshape(1, d), b.reshape(1, d))


def _oproj_ln_kernel(a_ref, w_ref, x_ref, g_ref, b_ref, o_ref, *, alpha):
    y = alpha * x_ref[...] + _dot(a_ref[...], w_ref[...])
    o_ref[...] = _layer_norm(y, g_ref[...], b_ref[...])


def _oproj_ln(a, w, layer, x, g, b, alpha):
    n, d = x.shape
    ow = a.shape[1]
    tm = min(512, n)
    return pl.pallas_call(
        functools.partial(_oproj_ln_kernel, alpha=alpha),
        grid=(n // tm,),
        in_specs=[
            pl.BlockSpec((tm, ow), lambda i: (i, 0)),
            _resident((None, ow, d), lambda i: (layer, 0, 0)),
            pl.BlockSpec((tm, d), lambda i: (i, 0)),
            pl.BlockSpec((1, d), lambda i: (0, 0)),
            pl.BlockSpec((1, d), lambda i: (0, 0)),
        ],
        out_specs=pl.BlockSpec((tm, d), lambda i: (i, 0)),
        out_shape=jax.ShapeDtypeStruct((n, d), F32),
        compiler_params=_params(1),
        name="oproj_ln",
    )(a, w, x, g.reshape(1, d), b.reshape(1, d))


def _mem_kv_kernel(x_ref, w_ref, of_ref, ob_ref):
    y = _dot(x_ref[...].astype(BF16), w_ref[...])
    of_ref[...] = y
    ob_ref[...] = y.astype(BF16)


def _mem_kv(mem, w):
    n, d = mem.shape
    depth, _, c = w.shape
    return pl.pallas_call(
        _mem_kv_kernel,
        grid=(depth,),
        in_specs=[
            _resident((n, d), lambda l: (0, 0)),
            pl.BlockSpec((None, d, c), lambda l: (l, 0, 0)),
        ],
        out_specs=[
            pl.BlockSpec((None, n, c), lambda l: (l, 0, 0)),
            pl.BlockSpec((None, n, c), lambda l: (l, 0, 0)),
        ],
        out_shape=[jax.ShapeDtypeStruct((depth, n, c), F32),
                   jax.ShapeDtypeStruct((depth, n, c), BF16)],
        compiler_params=_params(1),
        name="mem_kv",
    )(mem, w)


def _rope_tables(pos):
    posf = pos.astype(F32)[:, None]

    def cs(half):
        inv_freq = ROPE_THETA ** (-jnp.arange(half, dtype=F32) / half)
        ang = posf * inv_freq[None, :]
        return jnp.cos(ang), jnp.sin(ang)

    c64, s64 = cs(HEAD_DIM // 2)
    c32, s32 = cs(IDX_DIM // 2)
    z32 = jnp.zeros_like(s32)
    cos128 = jnp.concatenate([c64, c64], -1)
    sin128 = jnp.concatenate([-s64, s64], -1)
    cos64 = jnp.concatenate([c32] * 4, -1)
    sa64 = jnp.concatenate([-s32, z32, -s32, z32], -1)
    sb64 = jnp.concatenate([z32, s32, z32, s32], -1)
    lane = jnp.arange(LANES)[None, :]
    first = lane < IDX_DIM
    return jnp.stack([cos128, sin128, cos64, sa64, sb64,
                      jnp.where(first, cos64, 1.0), jnp.where(first, sa64, 0.0),
                      jnp.where(first, sb64, 0.0)])


def _proj_kernel(x_ref, w_ref, *refs, layer_a, emit_vt):
    xb = x_ref[...].astype(BF16)
    vt_ref = None
    if layer_a:
        tab_ref, q_ref, kf_ref, vf_ref, kb_ref, vb_ref, qm_ref, qi_ref, tail_ref, kid_ref = refs[:10]
        if emit_vt:
            vt_ref = refs[10]
    else:
        q_ref, kf_ref, vf_ref, kb_ref, vb_ref, qm_ref = refs

    def rope128(y):
        return y * tab_ref[0] + pltpu.roll(y, HEAD_DIM // 2, axis=1) * tab_ref[1]

    def rope64(y, t0):
        return (y * tab_ref[t0] + pltpu.roll(y, LANES - IDX_DIM // 2, axis=1) * tab_ref[t0 + 1]
                + pltpu.roll(y, IDX_DIM // 2, axis=1) * tab_ref[t0 + 2])

    def section(col0, width, fn):
        step = 512
        for c0 in range(0, width, step):
            cw = min(step, width - c0)
            y = _dot(xb, w_ref[:, col0 + c0:col0 + c0 + cw])
            for s in range(cw // LANES):
                fn(c0 + s * LANES, y[:, s * LANES:(s + 1) * LANES])

    def put(ref, fn=None):
        def f(off, y):
            ref[:, off:off + LANES] = (y if fn is None else fn(y)).astype(ref.dtype)
        return f

    def put2(ref_f, ref_b, fn=None):
        def f(off, y):
            y = y if fn is None else fn(y)
            ref_f[:, off // HEAD_DIM, :] = y
            ref_b[:, off:off + LANES] = y.astype(BF16)
        return f

    put_v = put2(vf_ref, vb_ref)

    def put_v_and_vt(off, y):
        put_v(off, y)
        g = off // HEAD_DIM
        for u in range(y.shape[0] // DSA_HALF):
            vt_ref[u, g, :HEAD_DIM, :] = y[u * DSA_HALF:(u + 1) * DSA_HALF, :].T.astype(BF16)
            vt_ref[u, g, HEAD_DIM:, :] = jnp.ones((V_ONES, DSA_HALF), BF16)

    rq = rope128 if layer_a else None
    section(0, Q_W, put(q_ref, rq))
    section(Q_W, KV_W, put2(kf_ref, kb_ref, rq))
    section(Q_W + KV_W, KV_W, put_v_and_vt if emit_vt else put_v)
    section(Q_W + 2 * KV_W, MQ_W, put(qm_ref))
    if layer_a:
        c = Q_W + 2 * KV_W + MQ_W
        section(c, IDXQ_W, put(qi_ref, lambda y: rope64(y, 2)))

        def tail(off, y):
            t = rope64(y, 5)
            tail_ref[...] = t
            lane = lax.broadcasted_iota(jnp.int32, t.shape, 1)
            dup = jnp.where(lane < IDX_DIM, t, pltpu.roll(t, IDX_DIM, axis=1))
            kid_ref[...] = dup.astype(BF16)
        section(c + IDXQ_W, LANES, tail)


def _proj(x, w, layer, tab, layer_a):
    n, d = x.shape
    pw = w.shape[2]
    tm = min(512, n)
    row = lambda i: (i, 0)
    in_specs = [pl.BlockSpec((tm, d), row), _resident((None, d, pw), lambda i: (layer, 0, 0))]
    args = [x, w]
    kv_state = (N_KV_HEADS, HEAD_DIM)
    outs = [((Q_W,), BF16), (kv_state, F32), (kv_state, F32), ((KV_W,), BF16), ((KV_W,), BF16),
            ((MQ_W,), BF16)]
    if layer_a:
        nt = tab.shape[1] // tm
        in_specs.append(pl.BlockSpec((8, tm, LANES), lambda i: (0, i % nt, 0)))
        args.append(tab)
        outs += [((IDXQ_W,), BF16), ((LANES,), F32), ((LANES,), BF16)]
    emit_vt = layer_a and tm % DSA_HALF == 0
    out_specs = [pl.BlockSpec((tm,) + wd, lambda i, nd=len(wd): (i,) + (0,) * nd) for wd, _ in outs]
    out_shape = [jax.ShapeDtypeStruct((n,) + wd, dt) for wd, dt in outs]
    if emit_vt:
        vt_tile = (N_KV_HEADS, HEAD_DIM + V_ONES, DSA_HALF)
        out_specs.append(pl.BlockSpec((tm // DSA_HALF,) + vt_tile, lambda i: (i, 0, 0, 0)))
        out_shape.append(jax.ShapeDtypeStruct((n // DSA_HALF,) + vt_tile, BF16))
    return pl.pallas_call(
        functools.partial(_proj_kernel, layer_a=layer_a, emit_vt=emit_vt),
        grid=(n // tm,),
        in_specs=in_specs,
        out_specs=out_specs,
        out_shape=out_shape,
        compiler_params=_params(1),
        name="proj_a" if layer_a else "proj_b",
    )(*args)


def _stack_heads(q_ref, g):
    return jnp.concatenate(
        [q_ref[:, (GROUP * g + r) * HEAD_DIM:(GROUP * g + r + 1) * HEAD_DIM] for r in range(GROUP)],
        axis=0)


def _unstack_heads(o_ref, g, o, tq):
    for r in range(GROUP):
        h = GROUP * g + r
        o_ref[:, h * HEAD_DIM:(h + 1) * HEAD_DIM] = o[r * tq:(r + 1) * tq].astype(o_ref.dtype)


def _mem_attend(qm_ref, mk_ref, mv_ref, o_ref):
    for hm in range(N_MEM_HEADS):
        sl = slice(hm * HEAD_DIM, (hm + 1) * HEAD_DIM)
        s = _dot_nt(qm_ref[:, sl], mk_ref[:, sl]) * ATTN_SCALE
        m = jnp.max(s, axis=1, keepdims=True)
        p = jnp.exp(s - m)
        l = jnp.sum(p, axis=1, keepdims=True)
        om = _dot(p.astype(BF16), mv_ref[:, sl]) / l
        o_ref[:, Q_W + hm * HEAD_DIM:Q_W + (hm + 1) * HEAD_DIM] = om.astype(o_ref.dtype)


def _sort_key(x):
    bits = lax.bitcast_convert_type(x, jnp.int32)
    return bits ^ ((bits >> 31) & 0x7FFFFFFF)


def _dsa_kernel(q_ref, qm_ref, qi_ref, tail_ref, k_ref, vt_ref, kid_ref, mk_ref, mv_ref,
                o_ref, keys_ref, hi_ref, lo_ref, sa_ref, sb_ref, *, tq, tk, s_valid, pos0, topk):
    i = pl.program_id(1)
    row0 = pos0 + i * tq
    qpos = row0 + lax.broadcasted_iota(jnp.int32, (1, tq), 1)
    lim = jnp.minimum(((qpos >> CHUNK_SHIFT) + 1) * CHUNK, s_valid)
    lim_max = jnp.minimum((((row0 + tq - 1) >> CHUNK_SHIFT) + 1) * CHUNK, s_valid)
    n_kt = (lim_max + tk - 1) // tk

    lane = lax.broadcasted_iota(jnp.int32, (tq, LANES), 1)
    pairs = []
    for p in range(IDX_HEADS // 2):
        qp = qi_ref[:, p * LANES:(p + 1) * LANES]
        zero = jnp.zeros_like(qp)
        pairs.append(jnp.concatenate(
            [jnp.where(lane < IDX_DIM, qp, zero), jnp.where(lane >= IDX_DIM, qp, zero)], axis=0))
    wt = tail_ref[...].T[IDX_DIM:IDX_DIM + IDX_HEADS, :] * IDX_W_SCALE
    krow = lax.broadcasted_iota(jnp.int32, (tk, tq), 0)
    n_keys = keys_ref.shape[0] * tk

    def idx_body(kt, carry):
        kid = kid_ref[pl.ds(pl.multiple_of(kt * tk, tk), tk), :]
        acc = jnp.zeros((tk, tq), F32)
        for p in range(IDX_HEADS // 2):
            rel = jnp.maximum(_dot_nt(kid, pairs[p]), 0.0)
            acc = acc + wt[2 * p:2 * p + 1, :] * rel[:, :tq]
            acc = acc + wt[2 * p + 1:2 * p + 2, :] * rel[:, tq:]
        sidx = krow + kt * tk
        score = jnp.where(sidx < lim, acc, NEG_INF)
        key = jnp.where(score == 0.0, n_keys - sidx, _sort_key(score))
        keys_ref[kt] = key
        hi_ref[kt] = (key >> 16).astype(jnp.int16)
        lo_ref[kt] = (key ^ 0x8000).astype(jnp.int16)
        return carry

    lax.fori_loop(0, n_kt, idx_body, 0)

    n_part = 4
    sub16 = 16

    def count16(ref, pred, cand):
        candb = jnp.broadcast_to(cand.astype(jnp.int16), (sub16, tq))

        def body(kt, cnts):
            xx = ref[kt]
            cnts = list(cnts)
            for r in range(tk // sub16):
                hit = pred(xx[r * sub16:(r + 1) * sub16, :], candb).astype(jnp.int16)
                cnts[r % n_part] = cnts[r % n_part] + hit
            return tuple(cnts)

        zero = jnp.zeros((sub16, tq), jnp.int16)
        cnts = lax.fori_loop(0, n_kt, body, (zero,) * n_part)
        total = sum(c.astype(F32) for c in cnts)
        return jnp.sum(total, axis=0, keepdims=True)

    ge = lambda a, b: a >= b
    gt = lambda a, b: a > b
    i16_min = -(2 ** 15)

    def search16(ref, need):
        zeros = jnp.zeros((1, tq), jnp.int32)
        thr0 = jnp.where(count16(ref, ge, zeros) >= need, zeros, jnp.full_like(zeros, i16_min))

        def step(it, thr):
            cand = thr | lax.shift_left(jnp.int32(1), 14 - it)
            return jnp.where(count16(ref, ge, cand) >= need, cand, thr)

        return lax.fori_loop(0, 15, step, thr0)

    need = jnp.full((1, tq), float(topk), F32)
    thr_hi = search16(hi_ref, need)
    need_lo = need - count16(hi_ref, gt, thr_hi)

    thr_hib = jnp.broadcast_to(thr_hi.astype(jnp.int16), (tk, tq))

    def keep_lo(kt, carry):
        lo_ref[kt] = jnp.where(hi_ref[kt] == thr_hib, lo_ref[kt], jnp.int16(i16_min))
        return carry

    lax.fori_loop(0, n_kt, keep_lo, 0)
    thr_lo = search16(lo_ref, need_lo)
    thr = (thr_hi << 16) | ((thr_lo ^ 0x8000) & 0xFFFF)
    thr_b = jnp.broadcast_to(thr, (tk, tq))

    def count32(pred):
        def body(kt, cnt):
            hit = jnp.where(pred(keys_ref[kt], kt * tk), 1, 0)
            for r in range(tk // 8):
                cnt = cnt + hit[r * 8:(r + 1) * 8, :]
            return cnt
        cnt = lax.fori_loop(0, n_kt, body, jnp.zeros((8, tq), jnp.int32))
        return jnp.sum(cnt.astype(F32), axis=0, keepdims=True)

    surplus = jnp.where(thr > NEG_INF_KEY, count32(lambda kk, k0: kk >= thr_b) - need, 0.0)

    max_surplus = jnp.max(surplus)

    @pl.when(max_surplus > 0.0)
    def _():
        idx_bits = (keys_ref.shape[0] * tk).bit_length()

        def by_peeling():
            def peel(j, bound):
                bound_b = jnp.broadcast_to(bound, (tk, tq))

                def body(kt, best):
                    idx = krow + kt * tk
                    tie = jnp.logical_and(keys_ref[kt] == thr_b, idx < bound_b)
                    cand = jnp.where(tie, idx, -1)
                    for r in range(tk // 8):
                        best = jnp.maximum(best, cand[r * 8:(r + 1) * 8, :])
                    return best

                best = lax.fori_loop(0, n_kt, body, jnp.full((8, tq), -1, jnp.int32))
                top = jnp.max(best.astype(F32), axis=0, keepdims=True).astype(jnp.int32)
                return jnp.where(surplus > j.astype(F32), top, bound)

            return lax.fori_loop(0, max_surplus.astype(jnp.int32), peel,
                                 jnp.full((1, tq), 2 ** idx_bits, jnp.int32))

        def by_bisection():
            keep = count32(lambda kk, k0: kk == thr_b) - surplus

            def step(it, bound):
                cand = bound | lax.shift_left(jnp.int32(1), idx_bits - 1 - it)
                cand_b = jnp.broadcast_to(cand, (tk, tq))
                below = count32(lambda kk, k0: jnp.logical_and(kk == thr_b, krow + k0 < cand_b))
                return jnp.where(below <= keep, cand, bound)

            return lax.fori_loop(0, idx_bits, step, jnp.zeros((1, tq), jnp.int32))

        bound = lax.cond(max_surplus <= MAX_PEEL_PASSES, by_peeling, by_bisection)
        bound_b = jnp.broadcast_to(bound, (tk, tq))

        def demote(kt, carry):
            kk = keys_ref[kt]
            drop = jnp.logical_and(kk == thr_b, krow + kt * tk >= bound_b)
            keys_ref[kt] = jnp.where(drop, thr_b - 1, kk)
            return carry

        lax.fori_loop(0, n_kt, demote, 0)

    thr = jnp.maximum(thr, NEG_INF_KEY + 1)

    q3 = [_stack_heads(q_ref, g) for g in range(N_KV_HEADS)]

    half = tk // 2
    thr_h = jnp.broadcast_to(thr, (half, tq))

    def qk_stage(kt, sub, buf_ref):
        off = pl.multiple_of(kt * tk + sub * half, half)
        kk = keys_ref[kt, sub * half:(sub + 1) * half, :]
        bias = jnp.where(kk >= thr_h, 0.0, NEG_INF)
        bias = jnp.concatenate([bias] * GROUP, axis=1)
        for g in range(N_KV_HEADS):
            hs = slice(g * HEAD_DIM, (g + 1) * HEAD_DIM)
            buf_ref[g] = _dot_nt(k_ref[pl.ds(off, half), hs], q3[g]) + bias

    def softmax_stage(kt, sub, buf_ref, carry):
        out = []
        for g in range(N_KV_HEADS):
            m, acc = carry[g]
            m_new = jnp.maximum(m, jnp.max(buf_ref[g], axis=0, keepdims=True))
            a = jnp.exp2((m - m_new) * EXP2_SCALE)
            p = jnp.exp2((buf_ref[g] - m_new) * EXP2_SCALE)
            acc = a * acc + _dot(vt_ref[2 * kt + sub, g], p.astype(BF16))
            out.append((m_new, acc))
        return tuple(out)

    def att_body(kt, carry):
        qk_stage(kt, 1, sb_ref)
        carry = softmax_stage(kt, 0, sa_ref, carry)
        qk_stage(jnp.minimum(kt + 1, n_kt - 1), 0, sa_ref)
        return softmax_stage(kt, 1, sb_ref, carry)

    init = tuple((jnp.full((1, GROUP * tq), NEG_INF, F32),
                  jnp.zeros((HEAD_DIM + V_ONES, GROUP * tq), F32)) for _ in range(N_KV_HEADS))
    qk_stage(0, 0, sa_ref)
    res = lax.fori_loop(0, n_kt, att_body, init)
    for g in range(N_KV_HEADS):
        acc = res[g][1]
        o_t = acc[:HEAD_DIM] / acc[HEAD_DIM:HEAD_DIM + 1]
        for r in range(GROUP):
            h = GROUP * g + r
            o_ref[:, h * HEAD_DIM:(h + 1) * HEAD_DIM] = o_t[:, r * tq:(r + 1) * tq].T.astype(o_ref.dtype)

    _mem_attend(qm_ref, mk_ref, mv_ref, o_ref)


def _dsa(q, qm, qi, tail, k, vt, kid, mk, mv, *, tq, s_valid, pos0, topk):
    n = q.shape[0]
    nb, n_half, _, vrows, half = vt.shape
    n_kt, tk = n_half // 2, 2 * half
    s_pad = k.shape[1]
    nq = n // nb // tq
    n_mem = mk.shape[0] // nb
    row = lambda b, i: (b * nq + i, 0)
    per_b = lambda b, i: (b, 0, 0)
    return pl.pallas_call(
        functools.partial(_dsa_kernel, tq=tq, tk=tk, s_valid=s_valid, pos0=pos0, topk=topk),
        grid=(nb, nq),
        in_specs=[
            pl.BlockSpec((tq, Q_W), row),
            pl.BlockSpec((tq, MQ_W), row),
            pl.BlockSpec((tq, IDXQ_W), row),
            pl.BlockSpec((tq, LANES), row),
            _resident((None, s_pad, KV_W), per_b),
            _resident((None, n_half, N_KV_HEADS, vrows, half), lambda b, i: (b, 0, 0, 0, 0)),
            _resident((None, s_pad, LANES), per_b),
            pl.BlockSpec((n_mem, MQ_W), lambda b, i: (b, 0)),
            pl.BlockSpec((n_mem, MQ_W), lambda b, i: (b, 0)),
        ],
        out_specs=pl.BlockSpec((tq, O_W), row),
        out_shape=jax.ShapeDtypeStruct((n, O_W), BF16),
        scratch_shapes=[pltpu.VMEM((n_kt, tk, tq), jnp.int32),
                        pltpu.VMEM((n_kt, tk, tq), jnp.int16),
                        pltpu.VMEM((n_kt, tk, tq), jnp.int16),
                        pltpu.VMEM((N_KV_HEADS, half, GROUP * tq), F32),
                        pltpu.VMEM((N_KV_HEADS, half, GROUP * tq), F32)],
        compiler_params=_params(2),
        name="dsa_attend",
    )(q, qm, qi, tail, k, vt, kid, mk, mv)


def _key_tiles_t(v, tk):
    nb, s, _ = v.shape
    vt = v.reshape(nb, s // tk, tk, N_KV_HEADS, HEAD_DIM).transpose(0, 1, 3, 4, 2)
    ones = jnp.ones((nb, s // tk, N_KV_HEADS, V_ONES, tk), v.dtype)
    return jnp.concatenate([vt, ones], axis=3)


def _band_table_kernel(rb_ref, o_ref, *, qc, band, prefix):
    h = pl.program_id(0)
    shape = o_ref.shape
    r = lax.broadcasted_iota(jnp.int32, shape, 0)
    w = lax.broadcasted_iota(jnp.int32, shape, 1)
    idx = jnp.clip(r - w + prefix, -REL_CLIP, REL_CLIP) + REL_CLIP
    lo = (r // qc) * qc
    inband = jnp.logical_and(w >= lo, w < lo + band)

    def body(t, acc):
        return jnp.where(idx == t, rb_ref[h, t], acc)

    acc = lax.fori_loop(0, 2 * REL_CLIP + 1, body, jnp.zeros(shape, F32))
    o_ref[...] = jnp.where(inband, acc, NEG_INF)


def _band_table(rel_bias, rows, width, qc, band, prefix):
    nh = rel_bias.shape[0]
    return pl.pallas_call(
        functools.partial(_band_table_kernel, qc=qc, band=band, prefix=prefix),
        grid=(nh,),
        in_specs=[pl.BlockSpec(memory_space=pltpu.SMEM)],
        out_specs=pl.BlockSpec((None, rows, width), lambda h: (h, 0, 0)),
        out_shape=jax.ShapeDtypeStruct((nh, rows, width), F32),
        compiler_params=_params(1),
        name="band_table",
    )(rel_bias)


def _band_kernel(q_ref, qm_ref, k_ref, v_ref, tab_ref, mk_ref, mv_ref, o_ref, *, tq, width, kp0):
    i = pl.program_id(1)
    off = pl.multiple_of(i * tq, tq)
    kp_ok = (kp0 + i * tq + lax.broadcasted_iota(jnp.int32, (GROUP * tq, width), 1)) >= 0
    for g in range(N_KV_HEADS):
        hs = slice(g * HEAD_DIM, (g + 1) * HEAD_DIM)
        q3 = _stack_heads(q_ref, g)
        s = _dot_nt(q3, k_ref[pl.ds(off, width), hs]) * ATTN_SCALE
        s = s + jnp.concatenate([tab_ref[GROUP * g + r] for r in range(GROUP)], axis=0)
        s = jnp.where(kp_ok, s, NEG_INF)
        m = jnp.max(s, axis=1, keepdims=True)
        p = jnp.exp(s - m)
        l = jnp.sum(p, axis=1, keepdims=True)
        o = _dot(p.astype(BF16), v_ref[pl.ds(off, width), hs]) / l
        _unstack_heads(o_ref, g, o, tq)
    _mem_attend(qm_ref, mk_ref, mv_ref, o_ref)


def _band(q, qm, k_all, v_all, table, mk, mv, *, tq, kp0):
    n = q.shape[0]
    nb, l_pad, _ = k_all.shape
    nq = n // nb // tq
    n_mem = mk.shape[0] // nb
    width = table.shape[2]
    row = lambda b, i: (b * nq + i, 0)
    per_b = lambda b, i: (b, 0, 0)
    return pl.pallas_call(
        functools.partial(_band_kernel, tq=tq, width=width, kp0=kp0),
        grid=(nb, nq),
        in_specs=[
            pl.BlockSpec((tq, Q_W), row),
            pl.BlockSpec((tq, MQ_W), row),
            _resident((None, l_pad, KV_W), per_b),
            _resident((None, l_pad, KV_W), per_b),
            _resident(table.shape, lambda b, i: (0, 0, 0)),
            pl.BlockSpec((n_mem, MQ_W), lambda b, i: (b, 0)),
            pl.BlockSpec((n_mem, MQ_W), lambda b, i: (b, 0)),
        ],
        out_specs=pl.BlockSpec((tq, O_W), row),
        out_shape=jax.ShapeDtypeStruct((n, O_W), BF16),
        compiler_params=_params(2),
        name="band_attend",
    )(q, qm, k_all, v_all, table, mk, mv)


def _pad_rows(a, rows):
    return jnp.pad(a, ((0, 0), (0, rows - a.shape[1]), (0, 0)))


def kernel(x_prompt, x_sample, mem_prompt, cache_a_k, cache_a_v, cache_a_kidx, cache_b_k, cache_b_v,
           cache_mem_k, cache_mem_v, w_in_a, w_in_b, w_o, rel_bias, w_mem_kv,
           w_ffn1_in, w_ffn1_out, w_ffn2_in, w_ffn2_out, ln_g, ln_b):
    depth = w_o.shape[0]
    alpha = (2 * depth) ** 0.25
    bp, tp, d = x_prompt.shape
    bs, ts, _ = x_sample.shape
    past = cache_a_k.shape[2]
    n_mem = mem_prompt.shape[1]
    assert tp % DSA_TK == 0 and ts % 16 == 0 and ts <= CHUNK
    assert cache_b_k.shape[2] == WIN_B and tp >= WIN_B

    pa_pad = -(-PROJ_A // LANES) * LANES
    wa = jnp.pad(w_in_a, ((0, 0), (0, 0), (0, pa_pad - PROJ_A))).astype(BF16)
    wb, wo = w_in_b.astype(BF16), w_o.astype(BF16)
    w1i, w1o = w_ffn1_in.astype(BF16), (0.5 * w_ffn1_out).astype(BF16)
    w2i, w2o = w_ffn2_in.astype(BF16), (0.5 * w_ffn2_out).astype(BF16)

    memf, memb = _mem_kv(mem_prompt.reshape(bp * n_mem, d), w_mem_kv.astype(BF16))
    mk_p = memf[..., :MQ_W].reshape(depth, bp, n_mem, N_MEM_HEADS, HEAD_DIM)
    mv_p = memf[..., MQ_W:].reshape(depth, bp, n_mem, N_MEM_HEADS, HEAD_DIM)
    mkb_p, mvb_p = memb[..., :MQ_W], memb[..., MQ_W:]
    mkb_s = cache_mem_k.astype(BF16).reshape(depth, bs * n_mem, MQ_W)
    mvb_s = cache_mem_v.astype(BF16).reshape(depth, bs * n_mem, MQ_W)

    tab_p = _rope_tables(jnp.arange(tp))
    tab_s = jnp.tile(_rope_tables(past + jnp.arange(ts)), (1, bs, 1))

    xp = x_prompt.reshape(bp * tp, d)
    xs = x_sample.reshape(bs * ts, d)
    st = {k: [] for k in ("akp", "avp", "aip", "bkp", "bvp", "aks", "avs", "ais", "bks", "bvs")}
    heads = lambda a, b, t: a.reshape(b, t, N_KV_HEADS, HEAD_DIM)

    for i in range(depth):
        j = i // 2
        g, b = ln_g[i], ln_b[i]
        xp = _ffn_ln(xp, w1i, w1o, i, g[0], b[0], alpha)
        xs = _ffn_ln(xs, w1i, w1o, i, g[0], b[0], alpha)
        if i % 2 == 0:
            q, kf, vf, kb, vb, qm, qi, tail, kid, vt = _proj(xp, wa, j, tab_p, True)
            ap = _dsa(q, qm, qi, tail, kb.reshape(bp, tp, KV_W),
                      vt.reshape((bp, tp // DSA_HALF) + vt.shape[1:]),
                      kid.reshape(bp, tp, LANES), mkb_p[i], mvb_p[i],
                      tq=min(TQ, tp), s_valid=tp, pos0=0, topk=min(TOPK_MAX, tp // 4))
            st["akp"].append(heads(kf, bp, tp))
            st["avp"].append(heads(vf, bp, tp))
            st["aip"].append(tail[:, :IDX_DIM].reshape(bp, tp, IDX_DIM))

            q, kf, vf, kb, vb, qm, qi, tail, kid = _proj(xs, wa, j, tab_s, True)
            tk = DSA_TK
            s_valid = past + ts
            s_pad = -(-s_valid // tk) * tk
            ci = cache_a_kidx[j].astype(BF16)
            k_all = _pad_rows(jnp.concatenate(
                [cache_a_k[j].astype(BF16).reshape(bs, past, KV_W), kb.reshape(bs, ts, KV_W)], 1), s_pad)
            v_all = _pad_rows(jnp.concatenate(
                [cache_a_v[j].astype(BF16).reshape(bs, past, KV_W), vb.reshape(bs, ts, KV_W)], 1), s_pad)
            kid_all = _pad_rows(jnp.concatenate(
                [jnp.concatenate([ci, ci], -1), kid.reshape(bs, ts, LANES)], 1), s_pad)
            tq = TQ_SAMPLE
            qpad = lambda a: _pad_rows(a.reshape(bs, ts, a.shape[-1]), tq).reshape(bs * tq, a.shape[-1])
            as_ = _dsa(qpad(q), qpad(qm), qpad(qi), qpad(tail), k_all, _key_tiles_t(v_all, tk // 2), kid_all,
                       mkb_s[i], mvb_s[i], tq=tq, s_valid=s_valid, pos0=past,
                       topk=min(TOPK_MAX, s_valid // 4))
            as_ = as_.reshape(bs, tq, O_W)[:, :ts].reshape(bs * ts, O_W)
            st["aks"].append(heads(kf, bs, ts))
            st["avs"].append(heads(vf, bs, ts))
            st["ais"].append(tail[:, :IDX_DIM].reshape(bs, ts, IDX_DIM))
        else:
            q, kf, vf, kb, vb, qm = _proj(xp, wb, j, None, False)
            tq = LANES
            table = _band_table(rel_bias[j], tq, tq + WIN_B, CHUNK, WIN_B + CHUNK, WIN_B)
            zpad = jnp.zeros((bp, WIN_B, KV_W), BF16)
            ap = _band(q, qm, jnp.concatenate([zpad, kb.reshape(bp, tp, KV_W)], 1),
                       jnp.concatenate([zpad, vb.reshape(bp, tp, KV_W)], 1),
                       table, mkb_p[i], mvb_p[i], tq=tq, kp0=-WIN_B)
            n_keep = min(WIN_B, tp)
            st["bkp"].append(heads(kf, bp, tp)[:, tp - n_keep:])
            st["bvp"].append(heads(vf, bp, tp)[:, tp - n_keep:])

            q, kf, vf, kb, vb, qm = _proj(xs, wb, j, None, False)
            width = -(-(WIN_B + ts) // LANES) * LANES
            table = _band_table(rel_bias[j], ts, width, ts, WIN_B + ts, WIN_B)
            k_all = _pad_rows(jnp.concatenate(
                [cache_b_k[j].astype(BF16).reshape(bs, WIN_B, KV_W), kb.reshape(bs, ts, KV_W)], 1), width)
            v_all = _pad_rows(jnp.concatenate(
                [cache_b_v[j].astype(BF16).reshape(bs, WIN_B, KV_W), vb.reshape(bs, ts, KV_W)], 1), width)
            as_ = _band(q, qm, k_all, v_all, table, mkb_s[i], mvb_s[i], tq=ts, kp0=past - WIN_B)
            st["bks"].append(jnp.concatenate([cache_b_k[j][:, ts:], heads(kf, bs, ts)], 1))
            st["bvs"].append(jnp.concatenate([cache_b_v[j][:, ts:], heads(vf, bs, ts)], 1))

        xp2 = _oproj_ln(ap, wo, i, xp, g[1], b[1], alpha)
        xs2 = _oproj_ln(as_, wo, i, xs, g[1], b[1], alpha)
        xp = _ffn_ln(xp2, w2i, w2o, i, g[2], b[2], alpha)
        xs = _ffn_ln(xs2, w2i, w2o, i, g[2], b[2], alpha)

    stack = lambda name: jnp.stack(st[name])
    return (xp.reshape(bp, tp, d), xs.reshape(bs, ts, d),
            stack("akp"), stack("avp"), stack("aip"), stack("bkp"), stack("bvp"), mk_p, mv_p,
            stack("aks"), stack("avs"), stack("ais"), stack("bks"), stack("bvs"))
```
